```python
import math
import jax
import jax.numpy as jnp
from jax import lax
import numpy as np

D_MODEL = 1024
BATCH = 8
SEQ = 4096
DEPTH = 4

N_EVEN = (DEPTH + 1) // 2
N_ODD = DEPTH // 2
EPS = 1e-6

GLA_HEADS = 4
GLA_DK = D_MODEL // 2
GLA_DV = D_MODEL
GLA_HK = GLA_DK // GLA_HEADS
GLA_HV = GLA_DV // GLA_HEADS
GLA_RANK = 16
GLA_TAU = 16.0
GLA_CHUNK = 64

CONV_WIDTH = D_MODEL
CONV_K = 31

S5_WIDTH = D_MODEL // 2
S5_GROUP = 16
S5_GROUPS = S5_WIDTH // S5_GROUP
S5_STATE = 64

SG_WIDTH = D_MODEL
SG_HEADS = 8
SG_HD = SG_WIDTH // SG_HEADS
SG_CHUNK = 128

EVEN_IN = 2 * GLA_DK + 2 * GLA_DV + GLA_RANK + 3 * CONV_WIDTH
EVEN_MIX = GLA_DV + CONV_WIDTH
ODD_IN = 2 * S5_WIDTH + 3 * SG_WIDTH
ODD_MIX = S5_WIDTH + SG_WIDTH

kernel_name = "hybrid_gla_conv_s5_gmlp_trunk"


def rmsnorm(x, g):
    xf = x.astype(jnp.float32)
    y = xf * lax.rsqrt(jnp.mean(xf * xf, axis=-1, keepdims=True) + EPS) * g.astype(jnp.float32)
    return y.astype(x.dtype)


def layernorm(x, g, b):
    xf = x.astype(jnp.float32)
    mu = jnp.mean(xf, axis=-1, keepdims=True)
    var = jnp.mean(jnp.square(xf - mu), axis=-1, keepdims=True)
    y = (xf - mu) * lax.rsqrt(var + EPS) * g.astype(jnp.float32) + b.astype(jnp.float32)
    return y.astype(x.dtype)


def gla_chunked(q, k, v, log_a):
    bsz, seq, nh, dk = q.shape
    dv = v.shape[-1]
    c = GLA_CHUNK
    n = seq // c

    def chunk(t):
        return t.reshape(bsz, n, c, nh, t.shape[-1]).transpose(0, 3, 1, 2, 4).astype(jnp.float32)

    q, k, v, g = chunk(q), chunk(k), chunk(v), chunk(log_a)
    b = jnp.cumsum(g, axis=3)
    b_last = b[:, :, :, -1:, :]
    qf = q * jnp.exp(b) * (dk ** -0.5)
    k_intra = k * jnp.exp(-b)
    k_state = k * jnp.exp(b_last - b)
    causal = jnp.tril(jnp.ones((c, c), dtype=bool))
    att = jnp.where(causal, jnp.einsum('bhnik,bhnjk->bhnij', qf, k_intra), 0.0)
    o_intra = jnp.einsum('bhnij,bhnjv->bhniv', att, v)
    chunk_kv = jnp.einsum('bhnjk,bhnjv->nbhkv', k_state, v)
    chunk_decay = jnp.exp(b_last[:, :, :, 0, :]).transpose(2, 0, 1, 3)

    def step(s, inp):
        kv, dec = inp
        return s * dec[..., None] + kv, s

    s0 = jnp.zeros((bsz, nh, dk, dv), jnp.float32)
    _, s_in = lax.scan(step, s0, (chunk_kv, chunk_decay))
    o = o_intra + jnp.einsum('bhnik,nbhkv->bhniv', qf, s_in)
    return o.transpose(0, 2, 3, 1, 4).reshape(bsz, seq, nh, dv)


def even_mixer(h, w_in, w_a2, b_a, gla_g, conv_w, conv_b, cln_g, cln_b, w_out):
    bsz, seq, _ = h.shape
    p = h @ w_in
    widths = [GLA_DK, GLA_DK, GLA_DV, GLA_DV, GLA_RANK, CONV_WIDTH, CONV_WIDTH, CONV_WIDTH]
    idx = np.cumsum(widths)[:-1].tolist()
    q, k, v, z_gla, a_low, c_val, c_gate, z_conv = jnp.split(p, idx, axis=-1)

    log_a = jax.nn.log_sigmoid((a_low @ w_a2 + b_a).astype(jnp.float32)) / GLA_TAU
    hd = lambda t, d: t.reshape(bsz, seq, GLA_HEADS, d)
    o = gla_chunked(hd(q, GLA_HK), hd(k, GLA_HK), hd(v, GLA_HV), hd(log_a, GLA_HK))
    o = o * lax.rsqrt(jnp.mean(o * o, axis=-1, keepdims=True) + EPS) * gla_g
    y_gla = o.reshape(bsz, seq, GLA_DV).astype(h.dtype) * jax.nn.silu(z_gla)

    u = c_val * jax.nn.sigmoid(c_gate)
    u = lax.conv_general_dilated(
        u, conv_w[:, None, :].astype(u.dtype), window_strides=(1,),
        padding=[(CONV_K - 1, 0)], dimension_numbers=('NWC', 'WIO', 'NWC'),
        feature_group_count=CONV_WIDTH) + conv_b
    u = jax.nn.silu(layernorm(u, cln_g, cln_b))
    y_conv = u * jax.nn.silu(z_conv)

    return jnp.concatenate([y_gla, y_conv], axis=-1) @ w_out


def s5_ssm(u, lam_re, lam_im, log_dt, b_re, b_im, c_re, c_im, d_skip):
    bsz, seq, _ = u.shape
    uf = u.astype(jnp.float32).reshape(bsz, seq, S5_GROUPS, S5_GROUP)
    dt = jnp.exp(log_dt.astype(jnp.float32))[:, None]
    mag = jnp.exp(lam_re * dt)
    abar_re = mag * jnp.cos(lam_im * dt)
    abar_im = mag * jnp.sin(lam_im * dt)
    den = lam_re * lam_re + lam_im * lam_im
    nr, ni = abar_re - 1.0, abar_im
    coef_re = (nr * lam_re + ni * lam_im) / den
    coef_im = (ni * lam_re - nr * lam_im) / den
    bbar_re = coef_re[..., None] * b_re - coef_im[..., None] * b_im
    bbar_im = coef_re[..., None] * b_im + coef_im[..., None] * b_re
    bu_re = jnp.einsum('blgh,gph->blgp', uf, bbar_re)
    bu_im = jnp.einsum('blgh,gph->blgp', uf, bbar_im)
    a_re = jnp.broadcast_to(abar_re, (1, seq, S5_GROUPS, S5_STATE))
    a_im = jnp.broadcast_to(abar_im, (1, seq, S5_GROUPS, S5_STATE))

    def combine(e1, e2):
        a1r, a1i, b1r, b1i = e1
        a2r, a2i, b2r, b2i = e2
        return (a1r * a2r - a1i * a2i,
                a1r * a2i + a1i * a2r,
                a2r * b1r - a2i * b1i + b2r,
                a2r * b1i + a2i * b1r + b2i)

    _, _, x_re, x_im = lax.associative_scan(combine, (a_re, a_im, bu_re, bu_im), axis=1)
    y = (jnp.einsum('blgp,ghp->blgh', x_re, c_re) - jnp.einsum('blgp,ghp->blgh', x_im, c_im)
         + d_skip.reshape(S5_GROUPS, S5_GROUP) * uf)
    return y.reshape(bsz, seq, S5_WIDTH)


def odd_mixer(h, w_in, lam_re, lam_im, log_dt, b_re, b_im, c_re, c_im, d_skip,
              w_glu, b_glu, sg_ln_g, sg_ln_b, w_s, b_s, w_out):
    bsz, seq, _ = h.shape
    p = h @ w_in
    widths = [S5_WIDTH, S5_WIDTH, SG_WIDTH, SG_WIDTH, SG_WIDTH]
    idx = np.cumsum(widths)[:-1].tolist()
    s5_u, s5_z, sg_u, sg_v, sg_z = jnp.split(p, idx, axis=-1)

    y = jax.nn.gelu(s5_ssm(s5_u, lam_re, lam_im, log_dt, b_re, b_im, c_re, c_im, d_skip))
    y = y * jax.nn.sigmoid(y @ w_glu + b_glu)
    y_s5 = y.astype(h.dtype) * jax.nn.silu(s5_z)

    n = seq // SG_CHUNK
    v = layernorm(sg_v, sg_ln_g, sg_ln_b).reshape(bsz, n, SG_CHUNK, SG_HEADS, SG_HD)
    causal = jnp.tril(jnp.ones((SG_CHUNK, SG_CHUNK), dtype=bool))
    ws = jnp.where(causal, w_s, 0.0)
    sv = jnp.einsum('hts,bnshc->bnthc', ws, v) + b_s.T[None, None, :, :, None]
    y_sg = sg_u * sv.reshape(bsz, seq, SG_WIDTH) * jax.nn.silu(sg_z)

    return jnp.concatenate([y_s5, y_sg.astype(h.dtype)], axis=-1) @ w_out


def setup_inputs(seed: int = 0) -> dict:
    key = jax.random.key(seed)
    ks = jax.random.split(key, 32)
    f32 = jnp.float32

    def nrm(k, shape, s):
        return jax.random.normal(k, shape, f32) * s

    n_idx = jnp.arange(S5_STATE, dtype=f32)
    return {
        "x": nrm(ks[0], (BATCH, SEQ, D_MODEL), 1.0),
        "norm_g": 1.0 + nrm(ks[1], (DEPTH, D_MODEL), 0.01),
        "final_g": 1.0 + nrm(ks[2], (D_MODEL,), 0.01),
        "e_w_in": nrm(ks[3], (N_EVEN, D_MODEL, EVEN_IN), D_MODEL ** -0.5),
        "e_w_a2": nrm(ks[4], (N_EVEN, GLA_RANK, GLA_DK), GLA_RANK ** -0.5),
        "e_b_a": nrm(ks[5], (N_EVEN, GLA_DK), 0.1),
        "e_gla_g": 1.0 + nrm(ks[6], (N_EVEN, GLA_HEADS, GLA_HV), 0.01),
        "e_conv_w": nrm(ks[7], (N_EVEN, CONV_K, CONV_WIDTH), CONV_K ** -0.5),
        "e_conv_b": nrm(ks[8], (N_EVEN, CONV_WIDTH), 0.01),
        "e_cln_g": 1.0 + nrm(ks[9], (N_EVEN, CONV_WIDTH), 0.01),
        "e_cln_b": nrm(ks[10], (N_EVEN, CONV_WIDTH), 0.01),
        "e_w_out": nrm(ks[11], (N_EVEN, EVEN_MIX, D_MODEL), EVEN_MIX ** -0.5),
        "o_w_in": nrm(ks[12], (N_ODD, D_MODEL, ODD_IN), D_MODEL ** -0.5),
        "o_lam_re": -0.5 + nrm(ks[13], (N_ODD, S5_GROUPS, S5_STATE), 0.01),
        "o_lam_im": math.pi * n_idx + nrm(ks[14], (N_ODD, S5_GROUPS, S5_STATE), 0.01),
        "o_log_dt": jax.random.uniform(ks[15], (N_ODD, S5_GROUPS), f32, math.log(1e-3), math.log(1e-1)),
        "o_b_re": nrm(ks[16], (N_ODD, S5_GROUPS, S5_STATE, S5_GROUP), (2 * S5_GROUP) ** -0.5),
        "o_b_im": nrm(ks[17], (N_ODD, S5_GROUPS, S5_STATE, S5_GROUP), (2 * S5_GROUP) ** -0.5),
        "o_c_re": nrm(ks[18], (N_ODD, S5_GROUPS, S5_GROUP, S5_STATE), S5_STATE ** -0.5),
        "o_c_im": nrm(ks[19], (N_ODD, S5_GROUPS, S5_GROUP, S5_STATE), S5_STATE ** -0.5),
        "o_d": nrm(ks[20], (N_ODD, S5_WIDTH), 1.0),
        "o_w_glu": nrm(ks[21], (N_ODD, S5_WIDTH, S5_WIDTH), S5_WIDTH ** -0.5),
        "o_b_glu": nrm(ks[22], (N_ODD, S5_WIDTH), 0.01),
        "o_sg_ln_g": 1.0 + nrm(ks[23], (N_ODD, SG_WIDTH), 0.01),
        "o_sg_ln_b": nrm(ks[24], (N_ODD, SG_WIDTH), 0.01),
        "o_w_s": nrm(ks[25], (N_ODD, SG_HEADS, SG_CHUNK, SG_CHUNK), 0.02),
        "o_b_s": 1.0 + nrm(ks[26], (N_ODD, SG_HEADS, SG_CHUNK), 0.01),
        "o_w_out": nrm(ks[27], (N_ODD, ODD_MIX, D_MODEL), ODD_MIX ** -0.5),
    }


def reference(x, norm_g, final_g, e_w_in, e_w_a2, e_b_a, e_gla_g, e_conv_w, e_conv_b,
              e_cln_g, e_cln_b, e_w_out, o_w_in, o_lam_re, o_lam_im, o_log_dt, o_b_re,
              o_b_im, o_c_re, o_c_im, o_d, o_w_glu, o_b_glu, o_sg_ln_g, o_sg_ln_b,
              o_w_s, o_b_s, o_w_out):
    h = x
    for layer in range(DEPTH):
        i = layer // 2
        hn = rmsnorm(h, norm_g[layer])
        if layer % 2 == 0:
            out = even_mixer(hn, e_w_in[i], e_w_a2[i], e_b_a[i], e_gla_g[i], e_conv_w[i],
                             e_conv_b[i], e_cln_g[i], e_cln_b[i], e_w_out[i])
        else:
            out = odd_mixer(hn, o_w_in[i], o_lam_re[i], o_lam_im[i], o_log_dt[i], o_b_re[i],
                            o_b_im[i], o_c_re[i], o_c_im[i], o_d[i], o_w_glu[i], o_b_glu[i],
                            o_sg_ln_g[i], o_sg_ln_b[i], o_w_s[i], o_b_s[i], o_w_out[i])
        h = h + out.astype(h.dtype)
    return rmsnorm(h, final_g)
```

```python
import functools
import math

import jax
import jax.numpy as jnp
from jax import lax
from jax.experimental import pallas as pl
from jax.experimental.pallas import tpu as pltpu

F32 = jnp.float32
BF16 = jnp.bfloat16

EPS = 1e-6
D_MODEL = 1024

GLA_HEADS = 4
GLA_DK = D_MODEL // 2
GLA_DV = D_MODEL
GLA_HK = GLA_DK // GLA_HEADS
GLA_HV = GLA_DV // GLA_HEADS
GLA_RANK = 16
GLA_TAU = 16.0
GLA_CHUNK = 64

CONV_WIDTH = D_MODEL
CONV_K = 31
CONV_HALO = 32

S5_WIDTH = D_MODEL // 2
S5_GROUP = 16
S5_GROUPS = S5_WIDTH // S5_GROUP
S5_STATE = 64
S5_LANES = S5_GROUPS * S5_STATE
S5_HALF_IN = S5_WIDTH // 2
S5_HALF_ST = S5_LANES // 2

SG_WIDTH = D_MODEL
SG_HEADS = 8
SG_HD = SG_WIDTH // SG_HEADS
SG_CHUNK = 128

LANE = 128
SUBLANE = 8
A_LOW_PAD = LANE
VMEM_LIMIT = 56 * 1024 * 1024


def _silu(x):
    return x * jax.nn.sigmoid(x)


def _rms_rows(x, g):
    return x * lax.rsqrt(jnp.mean(x * x, axis=-1, keepdims=True) + EPS) * g


def _layernorm_rows(x, g, b):
    mu = jnp.mean(x, axis=-1, keepdims=True)
    xc = x - mu
    var = jnp.mean(xc * xc, axis=-1, keepdims=True)
    return xc * lax.rsqrt(var + EPS) * g + b


def _log_sigmoid(x):
    return jnp.minimum(x, 0.0) - jnp.log1p(jnp.exp(-jnp.abs(x)))


def _gelu_tanh(x):
    c = math.sqrt(2.0 / math.pi)
    return 0.5 * x * (1.0 + jnp.tanh(c * (x + 0.044715 * (x * x * x))))


def _const_spec(shape):
    nd = len(shape)
    return pl.BlockSpec(shape, lambda *_: (0,) * nd)


E_Q0, E_K0, E_V0, E_Z0, E_A0 = 0, GLA_DK, 2 * GLA_DK, 2 * GLA_DK + GLA_DV, 2 * GLA_DK + 2 * GLA_DV
E_CV0 = E_A0 + A_LOW_PAD
E_CG0 = E_CV0 + CONV_WIDTH
E_CZ0 = E_CG0 + CONV_WIDTH
E_END = E_CZ0 + CONV_WIDTH


def _even_in_kernel(h_ref, g_ref, w_ref, wa2_ref, ba_ref,
                    q_ref, k_ref, v_ref, gz_ref, la_ref, u_ref, cz_ref):
    xn = _rms_rows(h_ref[...], g_ref[...]).astype(BF16)

    def proj(a, b):
        return jnp.dot(xn, w_ref[:, a:b], preferred_element_type=F32)

    q_ref[...] = proj(E_Q0, E_K0).astype(BF16)
    k_ref[...] = proj(E_K0, E_V0).astype(BF16)
    v_ref[...] = proj(E_V0, E_Z0).astype(BF16)
    gz_ref[...] = _silu(proj(E_Z0, E_A0)).astype(BF16)
    a_low = proj(E_A0, E_CV0).astype(BF16)
    logit = jnp.dot(a_low, wa2_ref[...], preferred_element_type=F32) + ba_ref[...]
    la_ref[...] = _log_sigmoid(logit) * (1.0 / GLA_TAU)
    c_val = proj(E_CV0, E_CG0)
    c_gate = proj(E_CG0, E_CZ0)
    u_ref[...] = (c_val * jax.nn.sigmoid(c_gate)).astype(BF16)
    cz_ref[...] = _silu(proj(E_CZ0, E_END)).astype(BF16)


def _even_in(h, g, w, wa2, ba, *, tm):
    t = h.shape[0]
    row = lambda c: pl.BlockSpec((tm, c), lambda i: (i, 0))
    outs = [(GLA_DK, BF16), (GLA_DK, BF16), (GLA_DV, BF16), (GLA_DV, BF16), (GLA_DK, F32),
            (CONV_WIDTH, BF16), (CONV_WIDTH, BF16)]
    return pl.pallas_call(
        _even_in_kernel,
        grid=(t // tm,),
        in_specs=[row(D_MODEL), _const_spec(g.shape), _const_spec(w.shape), _const_spec(wa2.shape),
                  _const_spec(ba.shape)],
        out_specs=[row(c) for c, _ in outs],
        out_shape=[jax.ShapeDtypeStruct((t, c), dt) for c, dt in outs],
        compiler_params=pltpu.CompilerParams(dimension_semantics=("arbitrary",),
                                             vmem_limit_bytes=VMEM_LIMIT),
        name="even_in",
    )(h, g, w, wa2, ba)


CONV_ROWS = 32
CONV_LANES = 256


def _even_mix_kernel(q_ref, k_ref, v_ref, gz_ref, la_ref, u_ref, cz_ref, h_ref,
                     glag_ref, cw_ref, cb_ref, lng_ref, lnb_ref, wout_ref,
                     o_ref, st_ref, y_ref, uext_ref, ush_ref, conv_ref, *, tt):
    @pl.when(pl.program_id(1) == 0)
    def _():
        st_ref[...] = jnp.zeros_like(st_ref)
        uext_ref[0:CONV_HALO, :] = jnp.zeros((CONV_HALO, CONV_WIDTH), F32)

    c = GLA_CHUNK
    rows = lax.broadcasted_iota(jnp.int32, (c, c), 0)
    cols = lax.broadcasted_iota(jnp.int32, (c, c), 1)
    causal = rows >= cols
    tri = causal.astype(F32)
    scale = GLA_HK ** -0.5

    def gla_chunk(ci, carry):
        r0 = pl.multiple_of(ci * c, c)
        for hd in range(GLA_HEADS):
            kl = slice(hd * GLA_HK, (hd + 1) * GLA_HK)
            vl = slice(hd * GLA_HV, (hd + 1) * GLA_HV)
            q = q_ref[pl.ds(r0, c), kl].astype(F32)
            k = k_ref[pl.ds(r0, c), kl].astype(F32)
            v = v_ref[pl.ds(r0, c), vl]
            g = la_ref[pl.ds(r0, c), kl]
            b = jnp.dot(tri, g, preferred_element_type=F32, precision=lax.Precision.HIGHEST)
            b_last = b[c - 1:c, :]
            qf = (q * jnp.exp(b) * scale).astype(BF16)
            k_intra = (k * jnp.exp(-b)).astype(BF16)
            k_state = (k * jnp.exp(b_last - b)).astype(BF16)
            att = lax.dot_general(qf, k_intra, (((1,), (1,)), ((), ())), preferred_element_type=F32)
            att = jnp.where(causal, att, 0.0).astype(BF16)
            s_t = st_ref[hd]
            o = jnp.dot(att, v, preferred_element_type=F32)
            o = o + lax.dot_general(qf, s_t.astype(BF16), (((1,), (1,)), ((), ())),
                                    preferred_element_type=F32)
            kv_t = lax.dot_general(v, k_state, (((0,), (0,)), ((), ())), preferred_element_type=F32)
            st_ref[hd] = s_t * jnp.exp(b_last) + kv_t
            o = o * lax.rsqrt(jnp.mean(o * o, axis=-1, keepdims=True) + EPS) * glag_ref[:, vl]
            y_ref[pl.ds(r0, c), vl] = (o * gz_ref[pl.ds(r0, c), vl].astype(F32)).astype(BF16)
        return carry

    lax.fori_loop(0, tt // c, gla_chunk, 0)

    uext_ref[CONV_HALO:CONV_HALO + tt, :] = u_ref[...].astype(F32)
    off0 = CONV_HALO - (CONV_K - 1)
    n_sh = tt + CONV_HALO - SUBLANE
    for r in range(1, SUBLANE):
        ush_ref[r - 1] = uext_ref[r:r + n_sh, :]

    def conv_block(rb, carry):
        r0 = pl.multiple_of(rb * CONV_ROWS, CONV_ROWS)
        for lb in range(CONV_WIDTH // CONV_LANES):
            ll = slice(lb * CONV_LANES, (lb + 1) * CONV_LANES)
            acc = jnp.zeros((CONV_ROWS, CONV_LANES), F32)
            for kk in range(CONV_K):
                a, r = divmod(off0 + kk, SUBLANE)
                rows = pl.ds(r0 + a * SUBLANE, CONV_ROWS)
                tap = uext_ref[rows, ll] if r == 0 else ush_ref[r - 1, rows, ll]
                acc = acc + cw_ref[kk:kk + 1, ll] * tap
            conv_ref[pl.ds(r0, CONV_ROWS), ll] = acc
        x = conv_ref[pl.ds(r0, CONV_ROWS), :] + cb_ref[...]
        x = _silu(_layernorm_rows(x, lng_ref[...], lnb_ref[...]))
        y_ref[pl.ds(r0, CONV_ROWS), GLA_DV:GLA_DV + CONV_WIDTH] = (
            x * cz_ref[pl.ds(r0, CONV_ROWS), :].astype(F32)).astype(BF16)
        return carry

    lax.fori_loop(0, tt // CONV_ROWS, conv_block, 0)
    uext_ref[0:CONV_HALO, :] = uext_ref[tt:tt + CONV_HALO, :]

    o_ref[...] = h_ref[...] + jnp.dot(y_ref[...], wout_ref[...], preferred_element_type=F32)


def _even_mix(q, k, v, gz, la, u, cz, h, glag, cw, cb, lng, lnb, wout, *, bsz, seq, tt):
    nt = seq // tt
    row = lambda c: pl.BlockSpec((tt, c), lambda b, i: (b * nt + i, 0))
    consts = [glag, cw, cb, lng, lnb, wout]
    return pl.pallas_call(
        functools.partial(_even_mix_kernel, tt=tt),
        grid=(bsz, nt),
        in_specs=[row(GLA_DK), row(GLA_DK), row(GLA_DV), row(GLA_DV), row(GLA_DK), row(CONV_WIDTH),
                  row(CONV_WIDTH), row(D_MODEL)] + [_const_spec(a.shape) for a in consts],
        out_specs=row(D_MODEL),
        out_shape=jax.ShapeDtypeStruct(h.shape, F32),
        scratch_shapes=[pltpu.VMEM((GLA_HEADS, GLA_HV, GLA_HK), F32),
                        pltpu.VMEM((tt, GLA_DV + CONV_WIDTH), BF16),
                        pltpu.VMEM((CONV_HALO + tt, CONV_WIDTH), F32),
                        pltpu.VMEM((SUBLANE - 1, tt + CONV_HALO - SUBLANE, CONV_WIDTH), F32),
                        pltpu.VMEM((tt, CONV_WIDTH), F32)],
        compiler_params=pltpu.CompilerParams(dimension_semantics=("arbitrary", "arbitrary"),
                                             vmem_limit_bytes=VMEM_LIMIT),
        name="even_mix",
    )(q, k, v, gz, la, u, cz, h, *consts)


O_SU0, O_SZ0, O_GU0 = 0, S5_WIDTH, 2 * S5_WIDTH
O_GV0 = O_GU0 + SG_WIDTH
O_GZ0 = O_GV0 + SG_WIDTH
O_END = O_GZ0 + SG_WIDTH


def _odd_in_kernel(h_ref, g_ref, w_ref, lng_ref, lnb_ref, su_ref, sz_ref, ug_ref, vn_ref):
    xn = _rms_rows(h_ref[...], g_ref[...]).astype(BF16)

    def proj(a, b):
        return jnp.dot(xn, w_ref[:, a:b], preferred_element_type=F32)

    su_ref[...] = proj(O_SU0, O_SZ0).astype(BF16)
    sz_ref[...] = _silu(proj(O_SZ0, O_GU0)).astype(BF16)
    sg_u = proj(O_GU0, O_GV0)
    sg_z = proj(O_GZ0, O_END)
    ug_ref[...] = (sg_u * _silu(sg_z)).astype(BF16)
    vn_ref[...] = _layernorm_rows(proj(O_GV0, O_GZ0), lng_ref[...], lnb_ref[...]).astype(BF16)


def _odd_in(h, g, w, lng, lnb, *, tm):
    t = h.shape[0]
    row = lambda c: pl.BlockSpec((tm, c), lambda i: (i, 0))
    outs = [S5_WIDTH, S5_WIDTH, SG_WIDTH, SG_WIDTH]
    return pl.pallas_call(
        _odd_in_kernel,
        grid=(t // tm,),
        in_specs=[row(D_MODEL), _const_spec(g.shape), _const_spec(w.shape), _const_spec(lng.shape),
                  _const_spec(lnb.shape)],
        out_specs=[row(c) for c in outs],
        out_shape=[jax.ShapeDtypeStruct((t, c), BF16) for c in outs],
        compiler_params=pltpu.CompilerParams(dimension_semantics=("arbitrary",),
                                             vmem_limit_bytes=VMEM_LIMIT),
        name="odd_in",
    )(h, g, w, lng, lnb)


S5_SCAN_LANES = 512


def _s5_kernel(u_ref, sz_ref, bre_ref, bim_ref, cre_ref, cim_ref, are_ref, aim_ref, d_ref,
               wglu_ref, bglu_ref, o_ref, xre_ref, xim_ref, sre_ref, sim_ref, *, bsz, tt):
    @pl.when(pl.program_id(0) == 0)
    def _():
        sre_ref[...] = jnp.zeros_like(sre_ref)
        sim_ref[...] = jnp.zeros_like(sim_ref)

    rows = bsz * tt
    u = u_ref[...].reshape(rows, S5_WIDTH)
    half_tiles = S5_HALF_ST // LANE
    for hf in range(2):
        ul = u[:, hf * S5_HALF_IN:(hf + 1) * S5_HALF_IN]
        bu_re = jnp.dot(ul, bre_ref[hf], preferred_element_type=F32)
        bu_im = jnp.dot(ul, bim_ref[hf], preferred_element_type=F32)
        for j in range(half_tiles):
            xre_ref[hf * half_tiles + j] = bu_re[:, j * LANE:(j + 1) * LANE]
            xim_ref[hf * half_tiles + j] = bu_im[:, j * LANE:(j + 1) * LANE]

    scan_tiles = S5_SCAN_LANES // LANE
    for cb in range(S5_LANES // S5_SCAN_LANES):
        tiles = range(cb * scan_tiles, (cb + 1) * scan_tiles)
        a_re = [are_ref[:, j * LANE:(j + 1) * LANE] for j in tiles]
        a_im = [aim_ref[:, j * LANE:(j + 1) * LANE] for j in tiles]

        def step(t, carry):
            idx = pl.ds(t, bsz, stride=tt)
            out = []
            for n, j in enumerate(tiles):
                x_re, x_im = carry[n]
                n_re = a_re[n] * x_re - a_im[n] * x_im + xre_ref[j, idx, :]
                n_im = a_re[n] * x_im + a_im[n] * x_re + xim_ref[j, idx, :]
                xre_ref[j, idx, :] = n_re
                xim_ref[j, idx, :] = n_im
                out.append((n_re, n_im))
            return tuple(out)

        init = tuple((sre_ref[:, j * LANE:(j + 1) * LANE], sim_ref[:, j * LANE:(j + 1) * LANE]) for j in tiles)
        fin = lax.fori_loop(0, tt, step, init)
        for n, j in enumerate(tiles):
            sre_ref[:, j * LANE:(j + 1) * LANE] = fin[n][0]
            sim_ref[:, j * LANE:(j + 1) * LANE] = fin[n][1]

    ys = []
    for hf in range(2):
        tl = range(hf * half_tiles, (hf + 1) * half_tiles)
        x_re = jnp.concatenate([xre_ref[j] for j in tl], axis=-1).astype(BF16)
        x_im = jnp.concatenate([xim_ref[j] for j in tl], axis=-1).astype(BF16)
        y = jnp.dot(x_re, cre_ref[hf], preferred_element_type=F32)
        y = y - jnp.dot(x_im, cim_ref[hf], preferred_element_type=F32)
        ys.append(y)
    y = jnp.concatenate(ys, axis=-1) + d_ref[...] * u.astype(F32)
    y = _gelu_tanh(y)
    y = y * jax.nn.sigmoid(jnp.dot(y.astype(BF16), wglu_ref[...], preferred_element_type=F32) + bglu_ref[...])
    y = y * sz_ref[...].reshape(rows, S5_WIDTH).astype(F32)
    o_ref[...] = y.astype(BF16).reshape(bsz, tt, S5_WIDTH)


def _s5(su, sz, bre, bim, cre, cim, are, aim, d, wglu, bglu, *, bsz, seq, tt):
    blk = pl.BlockSpec((bsz, tt, S5_WIDTH), lambda i: (0, i, 0))
    consts = [bre, bim, cre, cim, are, aim, d, wglu, bglu]
    return pl.pallas_call(
        functools.partial(_s5_kernel, bsz=bsz, tt=tt),
        grid=(seq // tt,),
        in_specs=[blk, blk] + [_const_spec(a.shape) for a in consts],
        out_specs=blk,
        out_shape=jax.ShapeDtypeStruct((bsz, seq, S5_WIDTH), BF16),
        scratch_shapes=[pltpu.VMEM((S5_LANES // LANE, bsz * tt, LANE), F32),
                        pltpu.VMEM((S5_LANES // LANE, bsz * tt, LANE), F32),
                        pltpu.VMEM((bsz, S5_LANES), F32), pltpu.VMEM((bsz, S5_LANES), F32)],
        compiler_params=pltpu.CompilerParams(dimension_semantics=("arbitrary",),
                                             vmem_limit_bytes=VMEM_LIMIT),
        name="s5",
    )(su, sz, *consts)


def _s5_params(lam_re, lam_im, log_dt, b_re, b_im, c_re, c_im, bsz):
    dt = jnp.exp(log_dt.astype(F32))[:, None]
    mag = jnp.exp(lam_re * dt)
    abar_re = mag * jnp.cos(lam_im * dt)
    abar_im = mag * jnp.sin(lam_im * dt)
    den = lam_re * lam_re + lam_im * lam_im
    nr, ni = abar_re - 1.0, abar_im
    coef_re = (nr * lam_re + ni * lam_im) / den
    coef_im = (ni * lam_re - nr * lam_im) / den
    bbar_re = coef_re[..., None] * b_re - coef_im[..., None] * b_im
    bbar_im = coef_re[..., None] * b_im + coef_im[..., None] * b_re
    eye = jnp.eye(S5_GROUPS, dtype=F32)

    def in_map(bb):
        full = jnp.einsum('gph,gk->ghkp', bb, eye).reshape(S5_WIDTH, S5_LANES)
        return jnp.stack([full[:S5_HALF_IN, :S5_HALF_ST], full[S5_HALF_IN:, S5_HALF_ST:]]).astype(BF16)

    def out_map(cc):
        full = jnp.einsum('ghp,gk->gpkh', cc, eye).reshape(S5_LANES, S5_WIDTH)
        return jnp.stack([full[:S5_HALF_ST, :S5_HALF_IN], full[S5_HALF_ST:, S5_HALF_IN:]]).astype(BF16)

    bcast = lambda a: jnp.broadcast_to(a.reshape(1, S5_LANES), (bsz, S5_LANES))
    return in_map(bbar_re), in_map(bbar_im), out_map(c_re), out_map(c_im), bcast(abar_re), bcast(abar_im)


def _odd_mix_kernel(ys_ref, ug_ref, vn_ref, h_ref, ws_ref, bs_ref, wout_ref, fg_ref, o_ref, y_ref,
                    *, tt, final_norm):
    y_ref[:, 0:S5_WIDTH] = ys_ref[...]
    for ci in range(tt // SG_CHUNK):
        rl = slice(ci * SG_CHUNK, (ci + 1) * SG_CHUNK)
        for hd in range(SG_HEADS):
            ll = slice(hd * SG_HD, (hd + 1) * SG_HD)
            sv = jnp.dot(ws_ref[hd], vn_ref[rl, ll], preferred_element_type=F32) + bs_ref[:, ll]
            y_ref[rl, S5_WIDTH + hd * SG_HD:S5_WIDTH + (hd + 1) * SG_HD] = (
                ug_ref[rl, ll].astype(F32) * sv).astype(BF16)
    out = h_ref[...] + jnp.dot(y_ref[...], wout_ref[...], preferred_element_type=F32)
    if final_norm:
        out = _rms_rows(out, fg_ref[...])
    o_ref[...] = out


def _odd_mix(ys, ug, vn, h, ws, bs, wout, fg, *, tt, final_norm):
    t = h.shape[0]
    row = lambda c: pl.BlockSpec((tt, c), lambda i: (i, 0))
    consts = [ws, bs, wout, fg]
    return pl.pallas_call(
        functools.partial(_odd_mix_kernel, tt=tt, final_norm=final_norm),
        grid=(t // tt,),
        in_specs=[row(S5_WIDTH), row(SG_WIDTH), row(SG_WIDTH), row(D_MODEL)]
                 + [_const_spec(a.shape) for a in consts],
        out_specs=row(D_MODEL),
        out_shape=jax.ShapeDtypeStruct(h.shape, F32),
        scratch_shapes=[pltpu.VMEM((tt, S5_WIDTH + SG_WIDTH), BF16)],
        compiler_params=pltpu.CompilerParams(dimension_semantics=("arbitrary",),
                                             vmem_limit_bytes=VMEM_LIMIT),
        name="odd_mix",
    )(ys, ug, vn, h, *consts)


def _even_layer(h, bsz, seq, norm_g, w_in, w_a2, b_a, gla_g, conv_w, conv_b, cln_g, cln_b, w_out):
    w = jnp.concatenate(
        [w_in[:, :E_A0], jnp.pad(w_in[:, E_A0:E_A0 + GLA_RANK], ((0, 0), (0, A_LOW_PAD - GLA_RANK))),
         w_in[:, E_A0 + GLA_RANK:]], axis=1).astype(BF16)
    wa2 = jnp.pad(w_a2, ((0, A_LOW_PAD - GLA_RANK), (0, 0))).astype(BF16)
    q, k, v, gz, la, u, cz = _even_in(h, norm_g.reshape(1, -1), w, wa2, b_a.reshape(1, -1), tm=256)
    cw = jnp.pad(conv_w, ((0, CONV_HALO - CONV_K), (0, 0)))
    return _even_mix(q, k, v, gz, la, u, cz, h, gla_g.reshape(1, -1), cw, conv_b.reshape(1, -1),
                     cln_g.reshape(1, -1), cln_b.reshape(1, -1), w_out.astype(BF16),
                     bsz=bsz, seq=seq, tt=256)


def _odd_layer(h, bsz, seq, norm_g, w_in, lam_re, lam_im, log_dt, b_re, b_im, c_re, c_im, d_skip,
               w_glu, b_glu, sg_ln_g, sg_ln_b, w_s, b_s, w_out, final_g, final_norm):
    su, sz, ug, vn = _odd_in(h, norm_g.reshape(1, -1), w_in.astype(BF16), sg_ln_g.reshape(1, -1),
                             sg_ln_b.reshape(1, -1), tm=256)
    bre, bim, cre, cim, are, aim = _s5_params(lam_re, lam_im, log_dt, b_re, b_im, c_re, c_im, bsz)
    ys = _s5(su.reshape(bsz, seq, S5_WIDTH), sz.reshape(bsz, seq, S5_WIDTH), bre, bim, cre, cim, are, aim,
             d_skip.reshape(1, -1), w_glu.astype(BF16), b_glu.reshape(1, -1), bsz=bsz, seq=seq, tt=64)
    causal = jnp.tril(jnp.ones((SG_CHUNK, SG_CHUNK), dtype=bool))
    ws = jnp.where(causal, w_s, 0.0).astype(BF16)
    bs = jnp.repeat(b_s.T, SG_HD, axis=1)
    return _odd_mix(ys.reshape(bsz * seq, S5_WIDTH), ug, vn, h, ws, bs, w_out.astype(BF16),
                    final_g.reshape(1, -1), tt=512, final_norm=final_norm)


def kernel(x, norm_g, final_g, e_w_in, e_w_a2, e_b_a, e_gla_g, e_conv_w, e_conv_b, e_cln_g, e_cln_b, e_w_out, o_w_in, o_lam_re, o_lam_im, o_log_dt, o_b_re, o_b_im, o_c_re, o_c_im, o_d, o_w_glu, o_b_glu, o_sg_ln_g, o_sg_ln_b, o_w_s, o_b_s, o_w_out):
    bsz, seq, d = x.shape
    assert d == D_MODEL and bsz == 8 and seq % 512 == 0
    depth = norm_g.shape[0]
    assert depth % 2 == 0
    h = x.reshape(bsz * seq, d)
    for layer in range(depth):
        i = layer // 2
        if layer % 2 == 0:
            h = _even_layer(h, bsz, seq, norm_g[layer], e_w_in[i], e_w_a2[i], e_b_a[i], e_gla_g[i], e_conv_w[i],
                            e_conv_b[i], e_cln_g[i], e_cln_b[i], e_w_out[i])
        else:
            h = _odd_layer(h, bsz, seq, norm_g[layer], o_w_in[i], o_lam_re[i], o_lam_im[i], o_log_dt[i], o_b_re[i],
                           o_b_im[i], o_c_re[i], o_c_im[i], o_d[i], o_w_glu[i], o_b_glu[i], o_sg_ln_g[i],
                           o_sg_ln_b[i], o_w_s[i], o_b_s[i], o_w_out[i], final_g, layer == depth - 1)
    return h.reshape(bsz, seq, d)
```

```python
import functools
import math

import jax
import jax.numpy as jnp
from jax import lax
from jax.experimental import pallas as pl
from jax.experimental.pallas import tpu as pltpu

F32 = jnp.float32
BF16 = jnp.bfloat16

EPS = 1e-6
D_MODEL = 1024

GLA_HEADS = 4
GLA_DK = D_MODEL // 2
GLA_DV = D_MODEL
GLA_HK = GLA_DK // GLA_HEADS
GLA_HV = GLA_DV // GLA_HEADS
GLA_RANK = 16
GLA_TAU = 16.0
GLA_CHUNK = 64

CONV_WIDTH = D_MODEL
CONV_K = 31
CONV_HALO = 32

S5_WIDTH = D_MODEL // 2
S5_GROUP = 16
S5_GROUPS = S5_WIDTH // S5_GROUP
S5_STATE = 64
S5_LANES = S5_GROUPS * S5_STATE
S5_HALF_IN = S5_WIDTH // 2
S5_HALF_ST = S5_LANES // 2

SG_WIDTH = D_MODEL
SG_HEADS = 8
SG_HD = SG_WIDTH // SG_HEADS
SG_CHUNK = 128

LANE = 128
SUBLANE = 8
A_LOW_PAD = LANE
VMEM_LIMIT = 56 * 1024 * 1024


def _silu(x):
    return x * jax.nn.sigmoid(x)


def _rms_rows(x, g):
    return x * lax.rsqrt(jnp.mean(x * x, axis=-1, keepdims=True) + EPS) * g


def _layernorm_rows(x, g, b):
    mu = jnp.mean(x, axis=-1, keepdims=True)
    xc = x - mu
    var = jnp.mean(xc * xc, axis=-1, keepdims=True)
    return xc * lax.rsqrt(var + EPS) * g + b


def _log_sigmoid(x):
    return jnp.minimum(x, 0.0) - jnp.log1p(jnp.exp(-jnp.abs(x)))


def _gelu_tanh(x):
    c = math.sqrt(2.0 / math.pi)
    return 0.5 * x * (1.0 + jnp.tanh(c * (x + 0.044715 * (x * x * x))))


def _const_spec(shape):
    nd = len(shape)
    return pl.BlockSpec(shape, lambda *_: (0,) * nd)


E_Q0, E_K0, E_V0, E_Z0, E_A0 = 0, GLA_DK, 2 * GLA_DK, 2 * GLA_DK + GLA_DV, 2 * GLA_DK + 2 * GLA_DV
E_CV0 = E_A0 + A_LOW_PAD
E_CG0 = E_CV0 + CONV_WIDTH
E_CZ0 = E_CG0 + CONV_WIDTH
E_END = E_CZ0 + CONV_WIDTH


def _even_in_kernel(h_ref, g_ref, w_ref, wa2_ref, ba_ref,
                    q_ref, k_ref, v_ref, gz_ref, la_ref, u_ref, cz_ref):
    xn = _rms_rows(h_ref[...], g_ref[...]).astype(BF16)

    def proj(a, b):
        return jnp.dot(xn, w_ref[:, a:b], preferred_element_type=F32)

    q_ref[...] = proj(E_Q0, E_K0).astype(BF16)
    k_ref[...] = proj(E_K0, E_V0).astype(BF16)
    v_ref[...] = proj(E_V0, E_Z0).astype(BF16)
    gz_ref[...] = _silu(proj(E_Z0, E_A0)).astype(BF16)
    a_low = proj(E_A0, E_CV0).astype(BF16)
    logit = jnp.dot(a_low, wa2_ref[...], preferred_element_type=F32) + ba_ref[...]
    la_ref[...] = _log_sigmoid(logit) * (1.0 / GLA_TAU)
    c_val = proj(E_CV0, E_CG0)
    c_gate = proj(E_CG0, E_CZ0)
    u_ref[...] = (c_val * jax.nn.sigmoid(c_gate)).astype(BF16)
    cz_ref[...] = _silu(proj(E_CZ0, E_END)).astype(BF16)


def _even_in(h, g, w, wa2, ba, *, tm):
    t = h.shape[0]
    row = lambda c: pl.BlockSpec((tm, c), lambda i: (i, 0))
    outs = [(GLA_DK, BF16), (GLA_DK, BF16), (GLA_DV, BF16), (GLA_DV, BF16), (GLA_DK, F32),
            (CONV_WIDTH, BF16), (CONV_WIDTH, BF16)]
    return pl.pallas_call(
        _even_in_kernel,
        grid=(t // tm,),
        in_specs=[row(D_MODEL), _const_spec(g.shape), _const_spec(w.shape), _const_spec(wa2.shape),
                  _const_spec(ba.shape)],
        out_specs=[row(c) for c, _ in outs],
        out_shape=[jax.ShapeDtypeStruct((t, c), dt) for c, dt in outs],
        compiler_params=pltpu.CompilerParams(dimension_semantics=("arbitrary",),
                                             vmem_limit_bytes=VMEM_LIMIT),
        name="even_in",
    )(h, g, w, wa2, ba)


CONV_ROWS = 32
CONV_LANES = 256


def _even_mix_kernel(q_ref, k_ref, v_ref, gz_ref, la_ref, u_ref, cz_ref, h_ref,
                     glag_ref, cw_ref, cb_ref, lng_ref, lnb_ref, wout_ref,
                     o_ref, st_ref, y_ref, uext_ref, ush_ref, conv_ref, *, tt):
    @pl.when(pl.program_id(1) == 0)
    def _():
        st_ref[...] = jnp.zeros_like(st_ref)
        uext_ref[0:CONV_HALO, :] = jnp.zeros((CONV_HALO, CONV_WIDTH), F32)

    c = GLA_CHUNK
    nc = tt // c
    rows = lax.broadcasted_iota(jnp.int32, (tt, tt), 0)
    cols = lax.broadcasted_iota(jnp.int32, (tt, tt), 1)
    causal = (rows >= cols) & ((rows // c) == (cols // c))
    tri = causal.astype(BF16)
    scale = GLA_HK ** -0.5
    nt_dims = (((1,), (1,)), ((), ()))
    tn_dims = (((0,), (0,)), ((), ()))

    g = la_ref[...]
    g_hi = g.astype(BF16)
    r1 = g - g_hi.astype(F32)
    g_mid = r1.astype(BF16)
    g_lo = (r1 - g_mid.astype(F32)).astype(BF16)
    b = (jnp.dot(tri, g_hi, preferred_element_type=F32) + jnp.dot(tri, g_mid, preferred_element_type=F32)
         + jnp.dot(tri, g_lo, preferred_element_type=F32))
    b_last = [b[ci * c + c - 1:ci * c + c, :] for ci in range(nc)]
    b_last_rows = jnp.concatenate([jnp.broadcast_to(bl, (c, GLA_DK)) for bl in b_last], axis=0)
    q = q_ref[...].astype(F32)
    k = k_ref[...].astype(F32)
    qf = (q * jnp.exp(b) * scale).astype(BF16)
    k_intra = (k * jnp.exp(-b)).astype(BF16)
    k_state = (k * jnp.exp(b_last_rows - b)).astype(BF16)
    decay = [jnp.exp(bl) for bl in b_last]

    for hd in range(GLA_HEADS):
        kl = slice(hd * GLA_HK, (hd + 1) * GLA_HK)
        vl = slice(hd * GLA_HV, (hd + 1) * GLA_HV)
        v = v_ref[:, vl]
        att = lax.dot_general(qf[:, kl], k_intra[:, kl], nt_dims, preferred_element_type=F32)
        att = jnp.where(causal, att, 0.0).astype(BF16)
        o_intra = jnp.dot(att, v, preferred_element_type=F32)
        s_t = st_ref[hd]
        o_inter = []
        for ci in range(nc):
            rl = slice(ci * c, (ci + 1) * c)
            o_inter.append(lax.dot_general(qf[rl, kl], s_t.astype(BF16), nt_dims, preferred_element_type=F32))
            kv_t = lax.dot_general(v[rl, :], k_state[rl, kl], tn_dims, preferred_element_type=F32)
            s_t = s_t * decay[ci][:, kl] + kv_t
        st_ref[hd] = s_t
        o = o_intra + jnp.concatenate(o_inter, axis=0)
        o = o * lax.rsqrt(jnp.mean(o * o, axis=-1, keepdims=True) + EPS) * glag_ref[:, vl]
        y_ref[:, vl] = (o * gz_ref[:, vl].astype(F32)).astype(BF16)

    uext_ref[CONV_HALO:CONV_HALO + tt, :] = u_ref[...].astype(F32)
    off0 = CONV_HALO - (CONV_K - 1)
    n_sh = tt + CONV_HALO - SUBLANE
    for r in range(1, SUBLANE):
        ush_ref[r - 1] = uext_ref[r:r + n_sh, :]

    def conv_block(rb, carry):
        r0 = pl.multiple_of(rb * CONV_ROWS, CONV_ROWS)
        for lb in range(CONV_WIDTH // CONV_LANES):
            ll = slice(lb * CONV_LANES, (lb + 1) * CONV_LANES)
            acc = jnp.zeros((CONV_ROWS, CONV_LANES), F32)
            for kk in range(CONV_K):
                a, r = divmod(off0 + kk, SUBLANE)
                rows = pl.ds(r0 + a * SUBLANE, CONV_ROWS)
                tap = uext_ref[rows, ll] if r == 0 else ush_ref[r - 1, rows, ll]
                acc = acc + cw_ref[kk:kk + 1, ll] * tap
            conv_ref[pl.ds(r0, CONV_ROWS), ll] = acc
        x = conv_ref[pl.ds(r0, CONV_ROWS), :] + cb_ref[...]
        x = _silu(_layernorm_rows(x, lng_ref[...], lnb_ref[...]))
        y_ref[pl.ds(r0, CONV_ROWS), GLA_DV:GLA_DV + CONV_WIDTH] = (
            x * cz_ref[pl.ds(r0, CONV_ROWS), :].astype(F32)).astype(BF16)
        return carry

    lax.fori_loop(0, tt // CONV_ROWS, conv_block, 0)
    uext_ref[0:CONV_HALO, :] = uext_ref[tt:tt + CONV_HALO, :]

    o_ref[...] = h_ref[...] + jnp.dot(y_ref[...], wout_ref[...], preferred_element_type=F32)


def _even_mix(q, k, v, gz, la, u, cz, h, glag, cw, cb, lng, lnb, wout, *, bsz, seq, tt):
    nt = seq // tt
    row = lambda c: pl.BlockSpec((tt, c), lambda b, i: (b * nt + i, 0))
    consts = [glag, cw, cb, lng, lnb, wout]
    return pl.pallas_call(
        functools.partial(_even_mix_kernel, tt=tt),
        grid=(bsz, nt),
        in_specs=[row(GLA_DK), row(GLA_DK), row(GLA_DV), row(GLA_DV), row(GLA_DK), row(CONV_WIDTH),
                  row(CONV_WIDTH), row(D_MODEL)] + [_const_spec(a.shape) for a in consts],
        out_specs=row(D_MODEL),
        out_shape=jax.ShapeDtypeStruct(h.shape, F32),
        scratch_shapes=[pltpu.VMEM((GLA_HEADS, GLA_HV, GLA_HK), F32),
                        pltpu.VMEM((tt, GLA_DV + CONV_WIDTH), BF16),
                        pltpu.VMEM((CONV_HALO + tt, CONV_WIDTH), F32),
                        pltpu.VMEM((SUBLANE - 1, tt + CONV_HALO - SUBLANE, CONV_WIDTH), F32),
                        pltpu.VMEM((tt, CONV_WIDTH), F32)],
        compiler_params=pltpu.CompilerParams(dimension_semantics=("arbitrary", "arbitrary"),
                                             vmem_limit_bytes=VMEM_LIMIT),
        name="even_mix",
    )(q, k, v, gz, la, u, cz, h, *consts)


O_SU0, O_SZ0, O_GU0 = 0, S5_WIDTH, 2 * S5_WIDTH
O_GV0 = O_GU0 + SG_WIDTH
O_GZ0 = O_GV0 + SG_WIDTH
O_END = O_GZ0 + SG_WIDTH


def _odd_in_kernel(h_ref, g_ref, w_ref, lng_ref, lnb_ref, su_ref, sz_ref, ug_ref, vn_ref):
    xn = _rms_rows(h_ref[...], g_ref[...]).astype(BF16)

    def proj(a, b):
        return jnp.dot(xn, w_ref[:, a:b], preferred_element_type=F32)

    su_ref[...] = proj(O_SU0, O_SZ0).astype(BF16)
    sz_ref[...] = _silu(proj(O_SZ0, O_GU0)).astype(BF16)
    sg_u = proj(O_GU0, O_GV0)
    sg_z = proj(O_GZ0, O_END)
    ug_ref[...] = (sg_u * _silu(sg_z)).astype(BF16)
    vn_ref[...] = _layernorm_rows(proj(O_GV0, O_GZ0), lng_ref[...], lnb_ref[...]).astype(BF16)


def _odd_in(h, g, w, lng, lnb, *, tm):
    t = h.shape[0]
    row = lambda c: pl.BlockSpec((tm, c), lambda i: (i, 0))
    outs = [S5_WIDTH, S5_WIDTH, SG_WIDTH, SG_WIDTH]
    return pl.pallas_call(
        _odd_in_kernel,
        grid=(t // tm,),
        in_specs=[row(D_MODEL), _const_spec(g.shape), _const_spec(w.shape), _const_spec(lng.shape),
                  _const_spec(lnb.shape)],
        out_specs=[row(c) for c in outs],
        out_shape=[jax.ShapeDtypeStruct((t, c), BF16) for c in outs],
        compiler_params=pltpu.CompilerParams(dimension_semantics=("arbitrary",),
                                             vmem_limit_bytes=VMEM_LIMIT),
        name="odd_in",
    )(h, g, w, lng, lnb)


S5_SCAN_LANES = 512
S5_SCAN_UNROLL = 4
S5_PERM_T = 32


def _s5_kernel(u_ref, sz_ref, perm_ref, permt_ref, bre_ref, bim_ref, cre_ref, cim_ref, are_ref, aim_ref, d_ref,
               wglu_ref, bglu_ref, o_ref, xre_ref, xim_ref, sre_ref, sim_ref, *, bsz, tt):
    @pl.when(pl.program_id(0) == 0)
    def _():
        sre_ref[...] = jnp.zeros_like(sre_ref)
        sim_ref[...] = jnp.zeros_like(sim_ref)

    pt = S5_PERM_T
    prow = bsz * pt
    u = jnp.concatenate(
        [jnp.dot(perm_ref[...], u_ref[:, s * pt:(s + 1) * pt, :].reshape(prow, S5_WIDTH),
                 preferred_element_type=F32).astype(BF16) for s in range(tt // pt)], axis=0)
    for hf in range(2):
        ul = u[:, hf * S5_HALF_IN:(hf + 1) * S5_HALF_IN]
        sl = slice(hf * S5_HALF_ST, (hf + 1) * S5_HALF_ST)
        xre_ref[:, sl] = jnp.dot(ul, bre_ref[hf], preferred_element_type=F32)
        xim_ref[:, sl] = jnp.dot(ul, bim_ref[hf], preferred_element_type=F32)

    for cb in range(S5_LANES // S5_SCAN_LANES):
        ll = slice(cb * S5_SCAN_LANES, (cb + 1) * S5_SCAN_LANES)
        a_re = are_ref[:, ll]
        a_im = aim_ref[:, ll]

        def step(t, carry):
            x_re, x_im = carry
            idx = pl.ds(pl.multiple_of(t * bsz, bsz), bsz)
            n_re = a_re * x_re - a_im * x_im + xre_ref[idx, ll]
            n_im = a_re * x_im + a_im * x_re + xim_ref[idx, ll]
            xre_ref[idx, ll] = n_re
            xim_ref[idx, ll] = n_im
            return n_re, n_im

        x_re, x_im = lax.fori_loop(0, tt, step, (sre_ref[:, ll], sim_ref[:, ll]), unroll=S5_SCAN_UNROLL)
        sre_ref[:, ll] = x_re
        sim_ref[:, ll] = x_im

    ys = []
    for hf in range(2):
        sl = slice(hf * S5_HALF_ST, (hf + 1) * S5_HALF_ST)
        y = jnp.dot(xre_ref[:, sl].astype(BF16), cre_ref[hf], preferred_element_type=F32)
        y = y - jnp.dot(xim_ref[:, sl].astype(BF16), cim_ref[hf], preferred_element_type=F32)
        ys.append(y)
    y = jnp.concatenate(ys, axis=-1) + d_ref[...] * u.astype(F32)
    y = _gelu_tanh(y)
    y = y * jax.nn.sigmoid(jnp.dot(y.astype(BF16), wglu_ref[...], preferred_element_type=F32) + bglu_ref[...])
    y = y.astype(BF16)
    for s in range(tt // pt):
        y_bt = jnp.dot(permt_ref[...], y[s * prow:(s + 1) * prow, :], preferred_element_type=F32)
        gate = sz_ref[:, s * pt:(s + 1) * pt, :].astype(F32)
        o_ref[:, s * pt:(s + 1) * pt, :] = (y_bt.reshape(bsz, pt, S5_WIDTH) * gate).astype(BF16)


def _s5(su, sz, bre, bim, cre, cim, are, aim, d, wglu, bglu, *, bsz, seq, tt):
    blk = pl.BlockSpec((bsz, tt, S5_WIDTH), lambda i: (0, i, 0))
    r = jnp.arange(bsz * S5_PERM_T)
    perm = (r[None, :] == ((r % bsz) * S5_PERM_T + r // bsz)[:, None]).astype(BF16)
    consts = [perm, perm.T, bre, bim, cre, cim, are, aim, d, wglu, bglu]
    return pl.pallas_call(
        functools.partial(_s5_kernel, bsz=bsz, tt=tt),
        grid=(seq // tt,),
        in_specs=[blk, blk] + [_const_spec(a.shape) for a in consts],
        out_specs=blk,
        out_shape=jax.ShapeDtypeStruct((bsz, seq, S5_WIDTH), BF16),
        scratch_shapes=[pltpu.VMEM((bsz * tt, S5_LANES), F32), pltpu.VMEM((bsz * tt, S5_LANES), F32),
                        pltpu.VMEM((bsz, S5_LANES), F32), pltpu.VMEM((bsz, S5_LANES), F32)],
        compiler_params=pltpu.CompilerParams(dimension_semantics=("arbitrary",),
                                             vmem_limit_bytes=VMEM_LIMIT),
        name="s5",
    )(su, sz, *consts)


def _s5_params(lam_re, lam_im, log_dt, b_re, b_im, c_re, c_im, bsz):
    dt = jnp.exp(log_dt.astype(F32))[:, None]
    mag = jnp.exp(lam_re * dt)
    abar_re = mag * jnp.cos(lam_im * dt)
    abar_im = mag * jnp.sin(lam_im * dt)
    den = lam_re * lam_re + lam_im * lam_im
    nr, ni = abar_re - 1.0, abar_im
    coef_re = (nr * lam_re + ni * lam_im) / den
    coef_im = (ni * lam_re - nr * lam_im) / den
    bbar_re = coef_re[..., None] * b_re - coef_im[..., None] * b_im
    bbar_im = coef_re[..., None] * b_im + coef_im[..., None] * b_re
    eye = jnp.eye(S5_GROUPS, dtype=F32)

    def in_map(bb):
        full = jnp.einsum('gph,gk->ghkp', bb, eye).reshape(S5_WIDTH, S5_LANES)
        return jnp.stack([full[:S5_HALF_IN, :S5_HALF_ST], full[S5_HALF_IN:, S5_HALF_ST:]]).astype(BF16)

    def out_map(cc):
        full = jnp.einsum('ghp,gk->gpkh', cc, eye).reshape(S5_LANES, S5_WIDTH)
        return jnp.stack([full[:S5_HALF_ST, :S5_HALF_IN], full[S5_HALF_ST:, S5_HALF_IN:]]).astype(BF16)

    bcast = lambda a: jnp.broadcast_to(a.reshape(1, S5_LANES), (bsz, S5_LANES))
    return in_map(bbar_re), in_map(bbar_im), out_map(c_re), out_map(c_im), bcast(abar_re), bcast(abar_im)


def _odd_mix_kernel(ys_ref, ug_ref, vn_ref, h_ref, ws_ref, bs_ref, wout_ref, fg_ref, o_ref, y_ref,
                    *, tt, final_norm):
    y_ref[:, 0:S5_WIDTH] = ys_ref[...]
    for ci in range(tt // SG_CHUNK):
        rl = slice(ci * SG_CHUNK, (ci + 1) * SG_CHUNK)
        for hd in range(SG_HEADS):
            ll = slice(hd * SG_HD, (hd + 1) * SG_HD)
            sv = jnp.dot(ws_ref[hd], vn_ref[rl, ll], preferred_element_type=F32) + bs_ref[:, ll]
            y_ref[rl, S5_WIDTH + hd * SG_HD:S5_WIDTH + (hd + 1) * SG_HD] = (
                ug_ref[rl, ll].astype(F32) * sv).astype(BF16)
    out = h_ref[...] + jnp.dot(y_ref[...], wout_ref[...], preferred_element_type=F32)
    if final_norm:
        out = _rms_rows(out, fg_ref[...])
    o_ref[...] = out


def _odd_mix(ys, ug, vn, h, ws, bs, wout, fg, *, tt, final_norm):
    t = h.shape[0]
    row = lambda c: pl.BlockSpec((tt, c), lambda i: (i, 0))
    consts = [ws, bs, wout, fg]
    return pl.pallas_call(
        functools.partial(_odd_mix_kernel, tt=tt, final_norm=final_norm),
        grid=(t // tt,),
        in_specs=[row(S5_WIDTH), row(SG_WIDTH), row(SG_WIDTH), row(D_MODEL)]
                 + [_const_spec(a.shape) for a in consts],
        out_specs=row(D_MODEL),
        out_shape=jax.ShapeDtypeStruct(h.shape, F32),
        scratch_shapes=[pltpu.VMEM((tt, S5_WIDTH + SG_WIDTH), BF16)],
        compiler_params=pltpu.CompilerParams(dimension_semantics=("arbitrary",),
                                             vmem_limit_bytes=VMEM_LIMIT),
        name="odd_mix",
    )(ys, ug, vn, h, *consts)


def _even_layer(h, bsz, seq, norm_g, w_in, w_a2, b_a, gla_g, conv_w, conv_b, cln_g, cln_b, w_out):
    w = jnp.concatenate(
        [w_in[:, :E_A0], jnp.pad(w_in[:, E_A0:E_A0 + GLA_RANK], ((0, 0), (0, A_LOW_PAD - GLA_RANK))),
         w_in[:, E_A0 + GLA_RANK:]], axis=1).astype(BF16)
    wa2 = jnp.pad(w_a2, ((0, A_LOW_PAD - GLA_RANK), (0, 0))).astype(BF16)
    q, k, v, gz, la, u, cz = _even_in(h, norm_g.reshape(1, -1), w, wa2, b_a.reshape(1, -1), tm=256)
    cw = jnp.pad(conv_w, ((0, CONV_HALO - CONV_K), (0, 0)))
    return _even_mix(q, k, v, gz, la, u, cz, h, gla_g.reshape(1, -1), cw, conv_b.reshape(1, -1),
                     cln_g.reshape(1, -1), cln_b.reshape(1, -1), w_out.astype(BF16),
                     bsz=bsz, seq=seq, tt=256)


def _odd_layer(h, bsz, seq, norm_g, w_in, lam_re, lam_im, log_dt, b_re, b_im, c_re, c_im, d_skip,
               w_glu, b_glu, sg_ln_g, sg_ln_b, w_s, b_s, w_out, final_g, final_norm):
    su, sz, ug, vn = _odd_in(h, norm_g.reshape(1, -1), w_in.astype(BF16), sg_ln_g.reshape(1, -1),
                             sg_ln_b.reshape(1, -1), tm=256)
    bre, bim, cre, cim, are, aim = _s5_params(lam_re, lam_im, log_dt, b_re, b_im, c_re, c_im, bsz)
    ys = _s5(su.reshape(bsz, seq, S5_WIDTH), sz.reshape(bsz, seq, S5_WIDTH), bre, bim, cre, cim, are, aim,
             d_skip.reshape(1, -1), w_glu.astype(BF16), b_glu.reshape(1, -1), bsz=bsz, seq=seq, tt=64)
    causal = jnp.tril(jnp.ones((SG_CHUNK, SG_CHUNK), dtype=bool))
    ws = jnp.where(causal, w_s, 0.0).astype(BF16)
    bs = jnp.repeat(b_s.T, SG_HD, axis=1)
    return _odd_mix(ys.reshape(bsz * seq, S5_WIDTH), ug, vn, h, ws, bs, w_out.astype(BF16),
                    final_g.reshape(1, -1), tt=512, final_norm=final_norm)


def kernel(x, norm_g, final_g, e_w_in, e_w_a2, e_b_a, e_gla_g, e_conv_w, e_conv_b, e_cln_g, e_cln_b, e_w_out, o_w_in, o_lam_re, o_lam_im, o_log_dt, o_b_re, o_b_im, o_c_re, o_c_im, o_d, o_w_glu, o_b_glu, o_sg_ln_g, o_sg_ln_b, o_w_s, o_b_s, o_w_out):
    bsz, seq, d = x.shape
    assert d == D_MODEL and bsz == 8 and seq % 512 == 0
    depth = norm_g.shape[0]
    assert depth % 2 == 0
    h = x.reshape(bsz * seq, d)
    for layer in range(depth):
        i = layer // 2
        if layer % 2 == 0:
            h = _even_layer(h, bsz, seq, norm_g[layer], e_w_in[i], e_w_a2[i], e_b_a[i], e_gla_g[i], e_conv_w[i],
                            e_conv_b[i], e_cln_g[i], e_cln_b[i], e_w_out[i])
        else:
            h = _odd_layer(h, bsz, seq, norm_g[layer], o_w_in[i], o_lam_re[i], o_lam_im[i], o_log_dt[i], o_b_re[i],
                           o_b_im[i], o_c_re[i], o_c_im[i], o_d[i], o_w_glu[i], o_b_glu[i], o_sg_ln_g[i],
                           o_sg_ln_b[i], o_w_s[i], o_b_s[i], o_w_out[i], final_g, layer == depth - 1)
    return h.reshape(bsz, seq, d)
```

```python
import functools
import math

import jax
import jax.numpy as jnp
from jax import lax
from jax.experimental import pallas as pl
from jax.experimental.pallas import tpu as pltpu

F32 = jnp.float32
BF16 = jnp.bfloat16

EPS = 1e-6
D_MODEL = 1024

GLA_HEADS = 4
GLA_DK = D_MODEL // 2
GLA_DV = D_MODEL
GLA_HK = GLA_DK // GLA_HEADS
GLA_HV = GLA_DV // GLA_HEADS
GLA_RANK = 16
GLA_TAU = 16.0
GLA_CHUNK = 64

CONV_WIDTH = D_MODEL
CONV_K = 31
CONV_HALO = 32

S5_WIDTH = D_MODEL // 2
S5_GROUP = 16
S5_GROUPS = S5_WIDTH // S5_GROUP
S5_STATE = 64
S5_LANES = S5_GROUPS * S5_STATE
S5_HALF_IN = S5_WIDTH // 2
S5_HALF_ST = S5_LANES // 2

SG_WIDTH = D_MODEL
SG_HEADS = 8
SG_HD = SG_WIDTH // SG_HEADS
SG_CHUNK = 128

LANE = 128
SUBLANE = 8
A_LOW_PAD = LANE
VMEM_LIMIT = 56 * 1024 * 1024
IN_TM = 512
S5_TT = 64
ODD_TT = 512


def _silu(x):
    return x * jax.nn.sigmoid(x)


def _rms_rows(x, g):
    return x * lax.rsqrt(jnp.mean(x * x, axis=-1, keepdims=True) + EPS) * g


def _layernorm_rows(x, g, b):
    mu = jnp.mean(x, axis=-1, keepdims=True)
    xc = x - mu
    var = jnp.mean(xc * xc, axis=-1, keepdims=True)
    return xc * lax.rsqrt(var + EPS) * g + b


def _log_sigmoid(x):
    return jnp.minimum(x, 0.0) - jnp.log1p(jnp.exp(-jnp.abs(x)))


def _gelu_tanh(x):
    c = math.sqrt(2.0 / math.pi)
    return 0.5 * x * (1.0 + jnp.tanh(c * (x + 0.044715 * (x * x * x))))


def _const_spec(shape):
    nd = len(shape)
    return pl.BlockSpec(shape, lambda *_: (0,) * nd, pipeline_mode=pl.Buffered(1))


E_Q0, E_K0, E_V0, E_Z0, E_A0 = 0, GLA_DK, 2 * GLA_DK, 2 * GLA_DK + GLA_DV, 2 * GLA_DK + 2 * GLA_DV
E_CV0 = E_A0 + A_LOW_PAD
E_CG0 = E_CV0 + CONV_WIDTH
E_CZ0 = E_CG0 + CONV_WIDTH
E_END = E_CZ0 + CONV_WIDTH


def _even_in_kernel(h_ref, g_ref, w_ref, wa2_ref, ba_ref,
                    q_ref, k_ref, v_ref, gz_ref, la_ref, u_ref, cz_ref):
    xn = _rms_rows(h_ref[...], g_ref[...]).astype(BF16)

    def proj(a, b):
        return jnp.dot(xn, w_ref[:, a:b], preferred_element_type=F32)

    q_ref[...] = proj(E_Q0, E_K0).astype(BF16)
    k_ref[...] = proj(E_K0, E_V0).astype(BF16)
    v_ref[...] = proj(E_V0, E_Z0).astype(BF16)
    gz_ref[...] = _silu(proj(E_Z0, E_A0)).astype(BF16)
    a_low = proj(E_A0, E_CV0).astype(BF16)
    logit = jnp.dot(a_low, wa2_ref[...], preferred_element_type=F32) + ba_ref[...]
    la_ref[...] = _log_sigmoid(logit) * (1.0 / GLA_TAU)
    c_val = proj(E_CV0, E_CG0)
    c_gate = proj(E_CG0, E_CZ0)
    u_ref[...] = (c_val * jax.nn.sigmoid(c_gate)).astype(BF16)
    cz_ref[...] = _silu(proj(E_CZ0, E_END)).astype(BF16)


def _even_in(h, g, w, wa2, ba, *, tm):
    t = h.shape[0]
    row = lambda c: pl.BlockSpec((tm, c), lambda i: (i, 0))
    outs = [(GLA_DK, BF16), (GLA_DK, BF16), (GLA_DV, BF16), (GLA_DV, BF16), (GLA_DK, F32),
            (CONV_WIDTH, BF16), (CONV_WIDTH, BF16)]
    return pl.pallas_call(
        _even_in_kernel,
        grid=(t // tm,),
        in_specs=[row(D_MODEL), _const_spec(g.shape), _const_spec(w.shape), _const_spec(wa2.shape),
                  _const_spec(ba.shape)],
        out_specs=[row(c) for c, _ in outs],
        out_shape=[jax.ShapeDtypeStruct((t, c), dt) for c, dt in outs],
        compiler_params=pltpu.CompilerParams(dimension_semantics=("arbitrary",),
                                             vmem_limit_bytes=VMEM_LIMIT),
        name="even_in",
    )(h, g, w, wa2, ba)


CONV_ROWS = 64
CONV_SEG = 32
EVEN_TT = CONV_SEG * SUBLANE
CONV_EXT_ROWS = (CONV_HALO + CONV_SEG) * SUBLANE


def _conv_row_maps():
    te, seg = jnp.divmod(jnp.arange(CONV_EXT_ROWS), SUBLANE)
    src = seg * CONV_SEG + te - CONV_HALO
    sel_cur = (src[:, None] == jnp.arange(EVEN_TT)[None, :]).astype(BF16)
    sel_prev = ((src + CONV_HALO)[:, None] == jnp.arange(CONV_HALO)[None, :]).astype(BF16)
    ts, sg = jnp.divmod(jnp.arange(EVEN_TT), SUBLANE)
    unsel = (jnp.arange(EVEN_TT)[:, None] == (sg * CONV_SEG + ts)[None, :]).astype(BF16)
    return sel_cur, sel_prev, unsel


def _even_mix_kernel(q_ref, k_ref, v_ref, gz_ref, la_ref, u_ref, cz_ref, h_ref,
                     glag_ref, cw_ref, cb_ref, lng_ref, lnb_ref, wout_ref, selc_ref, selp_ref, unsel_ref,
                     o_ref, st_ref, y_ref, tail_ref, ext_ref, conv_ref, xb_ref, *, tt):
    @pl.when(pl.program_id(1) == 0)
    def _():
        st_ref[...] = jnp.zeros_like(st_ref)
        tail_ref[...] = jnp.zeros_like(tail_ref)

    c = GLA_CHUNK
    nc = tt // c
    rows = lax.broadcasted_iota(jnp.int32, (tt, tt), 0)
    cols = lax.broadcasted_iota(jnp.int32, (tt, tt), 1)
    causal = (rows >= cols) & ((rows // c) == (cols // c))
    tri = causal.astype(BF16)
    scale = GLA_HK ** -0.5
    nt_dims = (((1,), (1,)), ((), ()))
    tn_dims = (((0,), (0,)), ((), ()))

    g = la_ref[...]
    g_hi = g.astype(BF16)
    r1 = g - g_hi.astype(F32)
    g_mid = r1.astype(BF16)
    g_lo = (r1 - g_mid.astype(F32)).astype(BF16)
    b = (jnp.dot(tri, g_hi, preferred_element_type=F32) + jnp.dot(tri, g_mid, preferred_element_type=F32)
         + jnp.dot(tri, g_lo, preferred_element_type=F32))
    b_last = [b[ci * c + c - 1:ci * c + c, :] for ci in range(nc)]
    b_last_rows = jnp.concatenate([jnp.broadcast_to(bl, (c, GLA_DK)) for bl in b_last], axis=0)
    q = q_ref[...].astype(F32)
    k = k_ref[...].astype(F32)
    qf = (q * jnp.exp(b) * scale).astype(BF16)
    k_intra = (k * jnp.exp(-b)).astype(BF16)
    k_state = (k * jnp.exp(b_last_rows - b)).astype(BF16)
    decay = [jnp.exp(bl) for bl in b_last]

    for hd in range(GLA_HEADS):
        kl = slice(hd * GLA_HK, (hd + 1) * GLA_HK)
        vl = slice(hd * GLA_HV, (hd + 1) * GLA_HV)
        v = v_ref[:, vl]
        att = lax.dot_general(qf[:, kl], k_intra[:, kl], nt_dims, preferred_element_type=F32)
        att = jnp.where(causal, att, 0.0).astype(BF16)
        o_intra = jnp.dot(att, v, preferred_element_type=F32)
        s_t = st_ref[hd]
        o_inter = []
        for ci in range(nc):
            rl = slice(ci * c, (ci + 1) * c)
            o_inter.append(lax.dot_general(qf[rl, kl], s_t.astype(BF16), nt_dims, preferred_element_type=F32))
            kv_t = lax.dot_general(v[rl, :], k_state[rl, kl], tn_dims, preferred_element_type=F32)
            s_t = s_t * decay[ci][:, kl] + kv_t
        st_ref[hd] = s_t
        o = o_intra + jnp.concatenate(o_inter, axis=0)
        o = o * lax.rsqrt(jnp.mean(o * o, axis=-1, keepdims=True) + EPS) * glag_ref[:, vl]
        y_ref[:, vl] = (o * gz_ref[:, vl].astype(F32)).astype(BF16)

    ext_ref[...] = (jnp.dot(selc_ref[...], u_ref[...], preferred_element_type=F32)
                    + jnp.dot(selp_ref[...], tail_ref[...], preferred_element_type=F32))
    tail_ref[...] = u_ref[tt - CONV_HALO:tt, :]
    off0 = CONV_HALO - (CONV_K - 1)
    rsub = CONV_ROWS // SUBLANE

    def conv_block(rb, carry):
        r0 = pl.multiple_of(rb * CONV_ROWS, CONV_ROWS)
        for lb in range(CONV_WIDTH // LANE):
            ll = slice(lb * LANE, (lb + 1) * LANE)
            w = [cw_ref[kk * SUBLANE:(kk + 1) * SUBLANE, ll] for kk in range(CONV_K)]
            acc = [jnp.zeros((SUBLANE, LANE), F32) for _ in range(rsub)]
            for e in range(CONV_K + rsub - 1):
                x = ext_ref[pl.ds(r0 + (off0 + e) * SUBLANE, SUBLANE), ll]
                for j in range(rsub):
                    if 0 <= e - j < CONV_K:
                        acc[j] = acc[j] + x * w[e - j]
            for j in range(rsub):
                conv_ref[pl.ds(r0 + j * SUBLANE, SUBLANE), ll] = acc[j]
        x = conv_ref[pl.ds(r0, CONV_ROWS), :] + cb_ref[...]
        xb_ref[pl.ds(r0, CONV_ROWS), :] = _silu(_layernorm_rows(x, lng_ref[...], lnb_ref[...])).astype(BF16)
        return carry

    lax.fori_loop(0, tt // CONV_ROWS, conv_block, 0)
    y_conv = jnp.dot(unsel_ref[...], xb_ref[...], preferred_element_type=F32)
    y_ref[:, GLA_DV:GLA_DV + CONV_WIDTH] = (y_conv * cz_ref[...].astype(F32)).astype(BF16)

    o_ref[...] = h_ref[...] + jnp.dot(y_ref[...], wout_ref[...], preferred_element_type=F32)


def _even_mix(q, k, v, gz, la, u, cz, h, glag, cw, cb, lng, lnb, wout, *, bsz, seq, tt):
    assert tt == EVEN_TT
    nt = seq // tt
    row = lambda c: pl.BlockSpec((tt, c), lambda b, i: (b * nt + i, 0))
    consts = [glag, cw, cb, lng, lnb, wout, *_conv_row_maps()]
    return pl.pallas_call(
        functools.partial(_even_mix_kernel, tt=tt),
        grid=(bsz, nt),
        in_specs=[row(GLA_DK), row(GLA_DK), row(GLA_DV), row(GLA_DV), row(GLA_DK), row(CONV_WIDTH),
                  row(CONV_WIDTH), row(D_MODEL)] + [_const_spec(a.shape) for a in consts],
        out_specs=row(D_MODEL),
        out_shape=jax.ShapeDtypeStruct(h.shape, F32),
        scratch_shapes=[pltpu.VMEM((GLA_HEADS, GLA_HV, GLA_HK), F32),
                        pltpu.VMEM((tt, GLA_DV + CONV_WIDTH), BF16),
                        pltpu.VMEM((CONV_HALO, CONV_WIDTH), BF16),
                        pltpu.VMEM((CONV_EXT_ROWS, CONV_WIDTH), F32),
                        pltpu.VMEM((tt, CONV_WIDTH), F32),
                        pltpu.VMEM((tt, CONV_WIDTH), BF16)],
        compiler_params=pltpu.CompilerParams(dimension_semantics=("arbitrary", "arbitrary"),
                                             vmem_limit_bytes=VMEM_LIMIT),
        name="even_mix",
    )(q, k, v, gz, la, u, cz, h, *consts)


O_SU0, O_SZ0, O_GU0 = 0, S5_WIDTH, 2 * S5_WIDTH
O_GV0 = O_GU0 + SG_WIDTH
O_GZ0 = O_GV0 + SG_WIDTH
O_END = O_GZ0 + SG_WIDTH


def _odd_in_kernel(h_ref, g_ref, w_ref, lng_ref, lnb_ref, su_ref, sz_ref, ug_ref, vn_ref):
    xn = _rms_rows(h_ref[...], g_ref[...]).astype(BF16)

    def proj(a, b):
        return jnp.dot(xn, w_ref[:, a:b], preferred_element_type=F32)

    su_ref[...] = proj(O_SU0, O_SZ0).astype(BF16)
    sz_ref[...] = _silu(proj(O_SZ0, O_GU0)).astype(BF16)
    sg_u = proj(O_GU0, O_GV0)
    sg_z = proj(O_GZ0, O_END)
    ug_ref[...] = (sg_u * _silu(sg_z)).astype(BF16)
    vn_ref[...] = _layernorm_rows(proj(O_GV0, O_GZ0), lng_ref[...], lnb_ref[...]).astype(BF16)


def _odd_in(h, g, w, lng, lnb, *, tm):
    t = h.shape[0]
    row = lambda c: pl.BlockSpec((tm, c), lambda i: (i, 0))
    outs = [S5_WIDTH, S5_WIDTH, SG_WIDTH, SG_WIDTH]
    return pl.pallas_call(
        _odd_in_kernel,
        grid=(t // tm,),
        in_specs=[row(D_MODEL), _const_spec(g.shape), _const_spec(w.shape), _const_spec(lng.shape),
                  _const_spec(lnb.shape)],
        out_specs=[row(c) for c in outs],
        out_shape=[jax.ShapeDtypeStruct((t, c), BF16) for c in outs],
        compiler_params=pltpu.CompilerParams(dimension_semantics=("arbitrary",),
                                             vmem_limit_bytes=VMEM_LIMIT),
        name="odd_in",
    )(h, g, w, lng, lnb)


S5_SCAN_LANES = 512
S5_PERM_T = 32


def _s5_kernel(u_ref, sz_ref, perm_ref, permt_ref, bre_ref, bim_ref, cre_ref, cim_ref, are_ref, aim_ref, d_ref,
               wglu_ref, bglu_ref, o_ref, xre_ref, xim_ref, sre_ref, sim_ref, *, bsz, tt):
    @pl.when(pl.program_id(0) == 0)
    def _():
        sre_ref[...] = jnp.zeros_like(sre_ref)
        sim_ref[...] = jnp.zeros_like(sim_ref)

    pt = S5_PERM_T
    prow = bsz * pt
    u = jnp.concatenate(
        [jnp.dot(perm_ref[...], u_ref[:, s * pt:(s + 1) * pt, :].reshape(prow, S5_WIDTH),
                 preferred_element_type=F32).astype(BF16) for s in range(tt // pt)], axis=0)
    for hf in range(2):
        ul = u[:, hf * S5_HALF_IN:(hf + 1) * S5_HALF_IN]
        sl = slice(hf * S5_HALF_ST, (hf + 1) * S5_HALF_ST)
        xre_ref[:, sl] = jnp.dot(ul, bre_ref[hf], preferred_element_type=F32)
        xim_ref[:, sl] = jnp.dot(ul, bim_ref[hf], preferred_element_type=F32)

    for cb in range(S5_LANES // S5_SCAN_LANES):
        ll = slice(cb * S5_SCAN_LANES, (cb + 1) * S5_SCAN_LANES)
        a_re = are_ref[:, ll]
        a_im = aim_ref[:, ll]

        x_re = sre_ref[:, ll]
        x_im = sim_ref[:, ll]
        for t in range(tt):
            idx = slice(t * bsz, (t + 1) * bsz)
            x_re, x_im = (a_re * x_re - a_im * x_im + xre_ref[idx, ll],
                          a_re * x_im + a_im * x_re + xim_ref[idx, ll])
            xre_ref[idx, ll] = x_re
            xim_ref[idx, ll] = x_im
        sre_ref[:, ll] = x_re
        sim_ref[:, ll] = x_im

    ys = []
    for hf in range(2):
        sl = slice(hf * S5_HALF_ST, (hf + 1) * S5_HALF_ST)
        y = jnp.dot(xre_ref[:, sl].astype(BF16), cre_ref[hf], preferred_element_type=F32)
        y = y - jnp.dot(xim_ref[:, sl].astype(BF16), cim_ref[hf], preferred_element_type=F32)
        ys.append(y)
    y = jnp.concatenate(ys, axis=-1) + d_ref[...] * u.astype(F32)
    y = _gelu_tanh(y)
    y = y * jax.nn.sigmoid(jnp.dot(y.astype(BF16), wglu_ref[...], preferred_element_type=F32) + bglu_ref[...])
    y = y.astype(BF16)
    for s in range(tt // pt):
        y_bt = jnp.dot(permt_ref[...], y[s * prow:(s + 1) * prow, :], preferred_element_type=F32)
        gate = sz_ref[:, s * pt:(s + 1) * pt, :].astype(F32)
        o_ref[:, s * pt:(s + 1) * pt, :] = (y_bt.reshape(bsz, pt, S5_WIDTH) * gate).astype(BF16)


def _s5(su, sz, bre, bim, cre, cim, are, aim, d, wglu, bglu, *, bsz, seq, tt):
    blk = pl.BlockSpec((bsz, tt, S5_WIDTH), lambda i: (0, i, 0))
    r = jnp.arange(bsz * S5_PERM_T)
    perm = (r[None, :] == ((r % bsz) * S5_PERM_T + r // bsz)[:, None]).astype(BF16)
    consts = [perm, perm.T, bre, bim, cre, cim, are, aim, d, wglu, bglu]
    return pl.pallas_call(
        functools.partial(_s5_kernel, bsz=bsz, tt=tt),
        grid=(seq // tt,),
        in_specs=[blk, blk] + [_const_spec(a.shape) for a in consts],
        out_specs=blk,
        out_shape=jax.ShapeDtypeStruct((bsz, seq, S5_WIDTH), BF16),
        scratch_shapes=[pltpu.VMEM((bsz * tt, S5_LANES), F32), pltpu.VMEM((bsz * tt, S5_LANES), F32),
                        pltpu.VMEM((bsz, S5_LANES), F32), pltpu.VMEM((bsz, S5_LANES), F32)],
        compiler_params=pltpu.CompilerParams(dimension_semantics=("arbitrary",),
                                             vmem_limit_bytes=VMEM_LIMIT),
        name="s5",
    )(su, sz, *consts)


def _s5_params(lam_re, lam_im, log_dt, b_re, b_im, c_re, c_im, bsz):
    dt = jnp.exp(log_dt.astype(F32))[:, None]
    mag = jnp.exp(lam_re * dt)
    abar_re = mag * jnp.cos(lam_im * dt)
    abar_im = mag * jnp.sin(lam_im * dt)
    den = lam_re * lam_re + lam_im * lam_im
    nr, ni = abar_re - 1.0, abar_im
    coef_re = (nr * lam_re + ni * lam_im) / den
    coef_im = (ni * lam_re - nr * lam_im) / den
    bbar_re = coef_re[..., None] * b_re - coef_im[..., None] * b_im
    bbar_im = coef_re[..., None] * b_im + coef_im[..., None] * b_re
    eye = jnp.eye(S5_GROUPS, dtype=F32)

    def in_map(bb):
        full = jnp.einsum('gph,gk->ghkp', bb, eye).reshape(S5_WIDTH, S5_LANES)
        return jnp.stack([full[:S5_HALF_IN, :S5_HALF_ST], full[S5_HALF_IN:, S5_HALF_ST:]]).astype(BF16)

    def out_map(cc):
        full = jnp.einsum('ghp,gk->gpkh', cc, eye).reshape(S5_LANES, S5_WIDTH)
        return jnp.stack([full[:S5_HALF_ST, :S5_HALF_IN], full[S5_HALF_ST:, S5_HALF_IN:]]).astype(BF16)

    bcast = lambda a: jnp.broadcast_to(a.reshape(1, S5_LANES), (bsz, S5_LANES))
    return in_map(bbar_re), in_map(bbar_im), out_map(c_re), out_map(c_im), bcast(abar_re), bcast(abar_im)


def _odd_mix_kernel(ys_ref, ug_ref, vn_ref, h_ref, ws_ref, bs_ref, wout_ref, fg_ref, o_ref, y_ref,
                    *, tt, final_norm):
    y_ref[:, 0:S5_WIDTH] = ys_ref[...]
    for ci in range(tt // SG_CHUNK):
        rl = slice(ci * SG_CHUNK, (ci + 1) * SG_CHUNK)
        for hd in range(SG_HEADS):
            ll = slice(hd * SG_HD, (hd + 1) * SG_HD)
            sv = jnp.dot(ws_ref[hd], vn_ref[rl, ll], preferred_element_type=F32) + bs_ref[:, ll]
            y_ref[rl, S5_WIDTH + hd * SG_HD:S5_WIDTH + (hd + 1) * SG_HD] = (
                ug_ref[rl, ll].astype(F32) * sv).astype(BF16)
    out = h_ref[...] + jnp.dot(y_ref[...], wout_ref[...], preferred_element_type=F32)
    if final_norm:
        out = _rms_rows(out, fg_ref[...])
    o_ref[...] = out


def _odd_mix(ys, ug, vn, h, ws, bs, wout, fg, *, tt, final_norm):
    t = h.shape[0]
    row = lambda c: pl.BlockSpec((tt, c), lambda i: (i, 0))
    consts = [ws, bs, wout, fg]
    return pl.pallas_call(
        functools.partial(_odd_mix_kernel, tt=tt, final_norm=final_norm),
        grid=(t // tt,),
        in_specs=[row(S5_WIDTH), row(SG_WIDTH), row(SG_WIDTH), row(D_MODEL)]
                 + [_const_spec(a.shape) for a in consts],
        out_specs=row(D_MODEL),
        out_shape=jax.ShapeDtypeStruct(h.shape, F32),
        scratch_shapes=[pltpu.VMEM((tt, S5_WIDTH + SG_WIDTH), BF16)],
        compiler_params=pltpu.CompilerParams(dimension_semantics=("arbitrary",),
                                             vmem_limit_bytes=VMEM_LIMIT),
        name="odd_mix",
    )(ys, ug, vn, h, *consts)


def _even_layer(h, bsz, seq, norm_g, w_in, w_a2, b_a, gla_g, conv_w, conv_b, cln_g, cln_b, w_out):
    w = jnp.concatenate(
        [w_in[:, :E_A0], jnp.pad(w_in[:, E_A0:E_A0 + GLA_RANK], ((0, 0), (0, A_LOW_PAD - GLA_RANK))),
         w_in[:, E_A0 + GLA_RANK:]], axis=1).astype(BF16)
    wa2 = jnp.pad(w_a2, ((0, A_LOW_PAD - GLA_RANK), (0, 0))).astype(BF16)
    q, k, v, gz, la, u, cz = _even_in(h, norm_g.reshape(1, -1), w, wa2, b_a.reshape(1, -1), tm=IN_TM)
    cw = jnp.repeat(conv_w, SUBLANE, axis=0)
    return _even_mix(q, k, v, gz, la, u, cz, h, gla_g.reshape(1, -1), cw, conv_b.reshape(1, -1),
                     cln_g.reshape(1, -1), cln_b.reshape(1, -1), w_out.astype(BF16),
                     bsz=bsz, seq=seq, tt=EVEN_TT)


def _odd_layer(h, bsz, seq, norm_g, w_in, lam_re, lam_im, log_dt, b_re, b_im, c_re, c_im, d_skip,
               w_glu, b_glu, sg_ln_g, sg_ln_b, w_s, b_s, w_out, final_g, final_norm):
    su, sz, ug, vn = _odd_in(h, norm_g.reshape(1, -1), w_in.astype(BF16), sg_ln_g.reshape(1, -1),
                             sg_ln_b.reshape(1, -1), tm=IN_TM)
    bre, bim, cre, cim, are, aim = _s5_params(lam_re, lam_im, log_dt, b_re, b_im, c_re, c_im, bsz)
    ys = _s5(su.reshape(bsz, seq, S5_WIDTH), sz.reshape(bsz, seq, S5_WIDTH), bre, bim, cre, cim, are, aim,
             d_skip.reshape(1, -1), w_glu.astype(BF16), b_glu.reshape(1, -1), bsz=bsz, seq=seq, tt=S5_TT)
    causal = jnp.tril(jnp.ones((SG_CHUNK, SG_CHUNK), dtype=bool))
    ws = jnp.where(causal, w_s, 0.0).astype(BF16)
    bs = jnp.repeat(b_s.T, SG_HD, axis=1)
    return _odd_mix(ys.reshape(bsz * seq, S5_WIDTH), ug, vn, h, ws, bs, w_out.astype(BF16),
                    final_g.reshape(1, -1), tt=ODD_TT, final_norm=final_norm)


def kernel(x, norm_g, final_g, e_w_in, e_w_a2, e_b_a, e_gla_g, e_conv_w, e_conv_b, e_cln_g, e_cln_b, e_w_out, o_w_in, o_lam_re, o_lam_im, o_log_dt, o_b_re, o_b_im, o_c_re, o_c_im, o_d, o_w_glu, o_b_glu, o_sg_ln_g, o_sg_ln_b, o_w_s, o_b_s, o_w_out):
    bsz, seq, d = x.shape
    assert d == D_MODEL and bsz == 8 and seq % 512 == 0
    depth = norm_g.shape[0]
    assert depth % 2 == 0
    h = x.reshape(bsz * seq, d)
    for layer in range(depth):
        i = layer // 2
        if layer % 2 == 0:
            h = _even_layer(h, bsz, seq, norm_g[layer], e_w_in[i], e_w_a2[i], e_b_a[i], e_gla_g[i], e_conv_w[i],
                            e_conv_b[i], e_cln_g[i], e_cln_b[i], e_w_out[i])
        else:
            h = _odd_layer(h, bsz, seq, norm_g[layer], o_w_in[i], o_lam_re[i], o_lam_im[i], o_log_dt[i], o_b_re[i],
                           o_b_im[i], o_c_re[i], o_c_im[i], o_d[i], o_w_glu[i], o_b_glu[i], o_sg_ln_g[i],
                           o_sg_ln_b[i], o_w_s[i], o_b_s[i], o_w_out[i], final_g, layer == depth - 1)
    return h.reshape(bsz, seq, d)
```

```python
import functools
import math

import jax
import jax.numpy as jnp
from jax import lax
from jax.experimental import pallas as pl
from jax.experimental.pallas import tpu as pltpu

F32 = jnp.float32
BF16 = jnp.bfloat16

EPS = 1e-6
D_MODEL = 1024

GLA_HEADS = 4
GLA_DK = D_MODEL // 2
GLA_DV = D_MODEL
GLA_HK = GLA_DK // GLA_HEADS
GLA_HV = GLA_DV // GLA_HEADS
GLA_RANK = 16
GLA_TAU = 16.0
GLA_CHUNK = 64

CONV_WIDTH = D_MODEL
CONV_K = 31
CONV_HALO = 32

S5_WIDTH = D_MODEL // 2
S5_GROUP = 16
S5_GROUPS = S5_WIDTH // S5_GROUP
S5_STATE = 64
S5_LANES = S5_GROUPS * S5_STATE
S5_HALF_IN = S5_WIDTH // 2
S5_HALF_ST = S5_LANES // 2

SG_WIDTH = D_MODEL
SG_HEADS = 8
SG_HD = SG_WIDTH // SG_HEADS
SG_CHUNK = 128

LANE = 128
SUBLANE = 8
A_LOW_PAD = LANE
VMEM_LIMIT = 56 * 1024 * 1024
IN_TM = 512
S5_TT = 64
ODD_TT = 512


def _silu(x):
    return x * jax.nn.sigmoid(x)


def _rms_rows(x, g):
    return x * lax.rsqrt(jnp.mean(x * x, axis=-1, keepdims=True) + EPS) * g


def _layernorm_rows(x, g, b):
    mu = jnp.mean(x, axis=-1, keepdims=True)
    xc = x - mu
    var = jnp.mean(xc * xc, axis=-1, keepdims=True)
    return xc * lax.rsqrt(var + EPS) * g + b


def _log_sigmoid(x):
    return jnp.minimum(x, 0.0) - jnp.log1p(jnp.exp(-jnp.abs(x)))


def _gelu_tanh(x):
    c = math.sqrt(2.0 / math.pi)
    return 0.5 * x * (1.0 + jnp.tanh(c * (x + 0.044715 * (x * x * x))))


def _const_spec(shape):
    nd = len(shape)
    return pl.BlockSpec(shape, lambda *_: (0,) * nd, pipeline_mode=pl.Buffered(1))


CONV_ROWS = 64
CONV_SEG = 32
EVEN_TT = CONV_SEG * SUBLANE
CONV_EXT_ROWS = (CONV_HALO + CONV_SEG) * SUBLANE
EVEN_GROUPS = EVEN_TT // CONV_ROWS
E_A0 = 2 * GLA_DK + 2 * GLA_DV
E_CV0 = E_A0 + GLA_RANK
E_CZ0 = E_CV0 + 2 * CONV_WIDTH
P_QK, P_V, P_Z, P_CZ = range(EVEN_GROUPS)


def _conv_row_maps():
    te, seg = jnp.divmod(jnp.arange(CONV_EXT_ROWS), SUBLANE)
    src = seg * CONV_SEG + te - CONV_HALO
    sel_cur = (src[:, None] == jnp.arange(EVEN_TT)[None, :]).astype(BF16)
    sel_prev = ((src + CONV_HALO)[:, None] == jnp.arange(CONV_HALO)[None, :]).astype(BF16)
    ts, sg = jnp.divmod(jnp.arange(EVEN_TT), SUBLANE)
    unsel = (jnp.arange(EVEN_TT)[:, None] == (sg * CONV_SEG + ts)[None, :]).astype(BF16)
    return sel_cur, sel_prev, unsel


def _even_kernel(h_ref, g_ref, wc_ref, wl_ref, wm_ref, wa2_ref, ba_ref, glag_ref, cw_ref, cb_ref, lng_ref,
                 lnb_ref, wout_ref, selc_ref, selp_ref, unsel_ref,
                 o_ref, st_ref, y_ref, tail_ref, ext_ref, conv_ref, xb_ref, xn_ref, p_ref, la_ref, *, tt):
    @pl.when(pl.program_id(1) == 0)
    def _():
        st_ref[...] = jnp.zeros_like(st_ref)
        tail_ref[...] = jnp.zeros_like(tail_ref)

    xn = _rms_rows(h_ref[...], g_ref[...]).astype(BF16)
    xn_ref[...] = xn

    c_val = jnp.dot(xn, wc_ref[:, 0:CONV_WIDTH], preferred_element_type=F32)
    c_gate = jnp.dot(xn, wc_ref[:, CONV_WIDTH:2 * CONV_WIDTH], preferred_element_type=F32)
    u = (c_val * jax.nn.sigmoid(c_gate)).astype(BF16)
    ext_ref[...] = (jnp.dot(selc_ref[...], u, preferred_element_type=F32)
                    + jnp.dot(selp_ref[...], tail_ref[...], preferred_element_type=F32))
    tail_ref[...] = u[tt - CONV_HALO:tt, :]

    a_low = jnp.dot(xn, wl_ref[...], preferred_element_type=F32).astype(BF16)
    logit = jnp.dot(a_low, wa2_ref[...], preferred_element_type=F32) + ba_ref[...]
    la_ref[...] = _log_sigmoid(logit) * (1.0 / GLA_TAU)

    off0 = CONV_HALO - (CONV_K - 1)
    rsub = CONV_ROWS // SUBLANE

    def conv_block(rb, carry):
        p_ref[rb] = jnp.dot(xn_ref[...], wm_ref[rb], preferred_element_type=F32).astype(BF16)
        r0 = pl.multiple_of(rb * CONV_ROWS, CONV_ROWS)
        for lb in range(CONV_WIDTH // LANE):
            ll = slice(lb * LANE, (lb + 1) * LANE)
            acc = [jnp.zeros((SUBLANE, LANE), F32) for _ in range(rsub)]
            for e in range(CONV_K + rsub - 1):
                x = ext_ref[pl.ds(r0 + (off0 + e) * SUBLANE, SUBLANE), ll]
                for j in range(rsub):
                    kk = e - j
                    if 0 <= kk < CONV_K:
                        acc[j] = acc[j] + x * cw_ref[kk * SUBLANE:(kk + 1) * SUBLANE, ll]
            for j in range(rsub):
                conv_ref[pl.ds(r0 + j * SUBLANE, SUBLANE), ll] = acc[j]
        x = conv_ref[pl.ds(r0, CONV_ROWS), :] + cb_ref[...]
        xb_ref[pl.ds(r0, CONV_ROWS), :] = _silu(_layernorm_rows(x, lng_ref[...], lnb_ref[...])).astype(BF16)
        return carry

    lax.fori_loop(0, EVEN_GROUPS, conv_block, 0)
    y_conv = jnp.dot(unsel_ref[...], xb_ref[...], preferred_element_type=F32)
    y_ref[:, GLA_DV:GLA_DV + CONV_WIDTH] = (y_conv * _silu(p_ref[P_CZ].astype(F32))).astype(BF16)

    c = GLA_CHUNK
    nc = tt // c
    rows = lax.broadcasted_iota(jnp.int32, (tt, tt), 0)
    cols = lax.broadcasted_iota(jnp.int32, (tt, tt), 1)
    causal = (rows >= cols) & ((rows // c) == (cols // c))
    tri = causal.astype(BF16)
    scale = GLA_HK ** -0.5
    nt_dims = (((1,), (1,)), ((), ()))
    tn_dims = (((0,), (0,)), ((), ()))

    g = la_ref[...]
    g_hi = g.astype(BF16)
    r1 = g - g_hi.astype(F32)
    g_mid = r1.astype(BF16)
    g_lo = (r1 - g_mid.astype(F32)).astype(BF16)
    b = (jnp.dot(tri, g_hi, preferred_element_type=F32) + jnp.dot(tri, g_mid, preferred_element_type=F32)
         + jnp.dot(tri, g_lo, preferred_element_type=F32))
    b_last = [b[ci * c + c - 1:ci * c + c, :] for ci in range(nc)]
    b_last_rows = jnp.concatenate([jnp.broadcast_to(bl, (c, GLA_DK)) for bl in b_last], axis=0)
    q = p_ref[P_QK, :, 0:GLA_DK].astype(F32)
    k = p_ref[P_QK, :, GLA_DK:2 * GLA_DK].astype(F32)
    qf = (q * jnp.exp(b) * scale).astype(BF16)
    k_intra = (k * jnp.exp(-b)).astype(BF16)
    k_state = (k * jnp.exp(b_last_rows - b)).astype(BF16)
    decay = [jnp.exp(bl) for bl in b_last]

    for hd in range(GLA_HEADS):
        kl = slice(hd * GLA_HK, (hd + 1) * GLA_HK)
        vl = slice(hd * GLA_HV, (hd + 1) * GLA_HV)
        v = p_ref[P_V, :, vl]
        att = lax.dot_general(qf[:, kl], k_intra[:, kl], nt_dims, preferred_element_type=F32)
        att = jnp.where(causal, att, 0.0).astype(BF16)
        o_intra = jnp.dot(att, v, preferred_element_type=F32)
        s_t = st_ref[hd]
        o_inter = []
        for ci in range(nc):
            rl = slice(ci * c, (ci + 1) * c)
            o_inter.append(lax.dot_general(qf[rl, kl], s_t.astype(BF16), nt_dims, preferred_element_type=F32))
            kv_t = lax.dot_general(v[rl, :], k_state[rl, kl], tn_dims, preferred_element_type=F32)
            s_t = s_t * decay[ci][:, kl] + kv_t
        st_ref[hd] = s_t
        o = o_intra + jnp.concatenate(o_inter, axis=0)
        o = o * lax.rsqrt(jnp.mean(o * o, axis=-1, keepdims=True) + EPS) * glag_ref[:, vl]
        y_ref[:, vl] = (o * _silu(p_ref[P_Z, :, vl].astype(F32))).astype(BF16)

    o_ref[...] = h_ref[...] + jnp.dot(y_ref[...], wout_ref[...], preferred_element_type=F32)


def _even_layer(h, consts, *, bsz, seq):
    tt = EVEN_TT
    nt = seq // tt
    row = pl.BlockSpec((tt, D_MODEL), lambda b, i: (b * nt + i, 0))
    return pl.pallas_call(
        functools.partial(_even_kernel, tt=tt),
        grid=(bsz, nt),
        in_specs=[row] + [_const_spec(a.shape) for a in consts],
        out_specs=row,
        out_shape=jax.ShapeDtypeStruct(h.shape, F32),
        scratch_shapes=[pltpu.VMEM((GLA_HEADS, GLA_HV, GLA_HK), F32),
                        pltpu.VMEM((tt, GLA_DV + CONV_WIDTH), BF16),
                        pltpu.VMEM((CONV_HALO, CONV_WIDTH), BF16),
                        pltpu.VMEM((CONV_EXT_ROWS, CONV_WIDTH), F32),
                        pltpu.VMEM((tt, CONV_WIDTH), F32),
                        pltpu.VMEM((tt, CONV_WIDTH), BF16),
                        pltpu.VMEM((tt, D_MODEL), BF16),
                        pltpu.VMEM((EVEN_GROUPS, tt, D_MODEL), BF16),
                        pltpu.VMEM((tt, GLA_DK), F32)],
        compiler_params=pltpu.CompilerParams(dimension_semantics=("arbitrary", "arbitrary"),
                                             vmem_limit_bytes=VMEM_LIMIT),
        name="even_layer",
    )(h, *consts)


def _even_params(norm_g, w_in, w_a2, b_a, gla_g, conv_w, conv_b, cln_g, cln_b, w_out):
    n = w_in.shape[0]
    assert EVEN_GROUPS == 4
    wm = jnp.stack([w_in[:, :, 0:2 * GLA_DK], w_in[:, :, 2 * GLA_DK:2 * GLA_DK + GLA_DV],
                    w_in[:, :, 2 * GLA_DK + GLA_DV:E_A0], w_in[:, :, E_CZ0:]], axis=1).astype(BF16)
    wl = jnp.pad(w_in[:, :, E_A0:E_CV0], ((0, 0), (0, 0), (0, A_LOW_PAD - GLA_RANK))).astype(BF16)
    wc = w_in[:, :, E_CV0:E_CZ0].astype(BF16)
    wa2 = jnp.pad(w_a2, ((0, 0), (0, A_LOW_PAD - GLA_RANK), (0, 0))).astype(BF16)
    cw = jnp.repeat(conv_w, SUBLANE, axis=1)
    wout = w_out.astype(BF16)
    maps = _conv_row_maps()
    vec = lambda a: a.reshape(n, 1, -1)
    per_layer = [vec(norm_g), wc, wl, wm, wa2, vec(b_a), vec(gla_g), cw, vec(conv_b), vec(cln_g), vec(cln_b), wout]
    return [[a[i] for a in per_layer] + list(maps) for i in range(n)]


O_SU0, O_SZ0, O_GU0 = 0, S5_WIDTH, 2 * S5_WIDTH
O_GV0 = O_GU0 + SG_WIDTH
O_GZ0 = O_GV0 + SG_WIDTH
O_END = O_GZ0 + SG_WIDTH


def _odd_in_kernel(h_ref, g_ref, w_ref, lng_ref, lnb_ref, su_ref, sz_ref, ug_ref, vn_ref):
    xn = _rms_rows(h_ref[...], g_ref[...]).astype(BF16)

    def proj(a, b):
        return jnp.dot(xn, w_ref[:, a:b], preferred_element_type=F32)

    su_ref[...] = proj(O_SU0, O_SZ0).astype(BF16)
    sz_ref[...] = _silu(proj(O_SZ0, O_GU0)).astype(BF16)
    sg_u = proj(O_GU0, O_GV0)
    sg_z = proj(O_GZ0, O_END)
    ug_ref[...] = (sg_u * _silu(sg_z)).astype(BF16)
    vn_ref[...] = _layernorm_rows(proj(O_GV0, O_GZ0), lng_ref[...], lnb_ref[...]).astype(BF16)


def _odd_in(h, g, w, lng, lnb, *, tm):
    t = h.shape[0]
    row = lambda c: pl.BlockSpec((tm, c), lambda i: (i, 0))
    outs = [S5_WIDTH, S5_WIDTH, SG_WIDTH, SG_WIDTH]
    return pl.pallas_call(
        _odd_in_kernel,
        grid=(t // tm,),
        in_specs=[row(D_MODEL), _const_spec(g.shape), _const_spec(w.shape), _const_spec(lng.shape),
                  _const_spec(lnb.shape)],
        out_specs=[row(c) for c in outs],
        out_shape=[jax.ShapeDtypeStruct((t, c), BF16) for c in outs],
        compiler_params=pltpu.CompilerParams(dimension_semantics=("arbitrary",),
                                             vmem_limit_bytes=VMEM_LIMIT),
        name="odd_in",
    )(h, g, w, lng, lnb)


S5_SCAN_LANES = 512
S5_PERM_T = 32


def _s5_kernel(u_ref, sz_ref, perm_ref, permt_ref, bre_ref, bim_ref, cre_ref, cim_ref, are_ref, aim_ref, d_ref,
               wglu_ref, bglu_ref, o_ref, xre_ref, xim_ref, sre_ref, sim_ref, *, bsz, tt):
    @pl.when(pl.program_id(0) == 0)
    def _():
        sre_ref[...] = jnp.zeros_like(sre_ref)
        sim_ref[...] = jnp.zeros_like(sim_ref)

    pt = S5_PERM_T
    prow = bsz * pt
    u = jnp.concatenate(
        [jnp.dot(perm_ref[...], u_ref[:, s * pt:(s + 1) * pt, :].reshape(prow, S5_WIDTH),
                 preferred_element_type=F32).astype(BF16) for s in range(tt // pt)], axis=0)
    for hf in range(2):
        ul = u[:, hf * S5_HALF_IN:(hf + 1) * S5_HALF_IN]
        sl = slice(hf * S5_HALF_ST, (hf + 1) * S5_HALF_ST)
        xre_ref[:, sl] = jnp.dot(ul, bre_ref[hf], preferred_element_type=F32)
        xim_ref[:, sl] = jnp.dot(ul, bim_ref[hf], preferred_element_type=F32)

    for cb in range(S5_LANES // S5_SCAN_LANES):
        ll = slice(cb * S5_SCAN_LANES, (cb + 1) * S5_SCAN_LANES)
        a_re = are_ref[:, ll]
        a_im = aim_ref[:, ll]
        x_re = sre_ref[:, ll]
        x_im = sim_ref[:, ll]
        for t in range(tt):
            idx = slice(t * bsz, (t + 1) * bsz)
            x_re, x_im = (a_re * x_re - a_im * x_im + xre_ref[idx, ll],
                          a_re * x_im + a_im * x_re + xim_ref[idx, ll])
            xre_ref[idx, ll] = x_re
            xim_ref[idx, ll] = x_im
        sre_ref[:, ll] = x_re
        sim_ref[:, ll] = x_im

    ys = []
    for hf in range(2):
        sl = slice(hf * S5_HALF_ST, (hf + 1) * S5_HALF_ST)
        y = jnp.dot(xre_ref[:, sl].astype(BF16), cre_ref[hf], preferred_element_type=F32)
        y = y - jnp.dot(xim_ref[:, sl].astype(BF16), cim_ref[hf], preferred_element_type=F32)
        ys.append(y)
    y = jnp.concatenate(ys, axis=-1) + d_ref[...] * u.astype(F32)
    y = _gelu_tanh(y)
    y = y * jax.nn.sigmoid(jnp.dot(y.astype(BF16), wglu_ref[...], preferred_element_type=F32) + bglu_ref[...])
    y = y.astype(BF16)
    for s in range(tt // pt):
        y_bt = jnp.dot(permt_ref[...], y[s * prow:(s + 1) * prow, :], preferred_element_type=F32)
        gate = sz_ref[:, s * pt:(s + 1) * pt, :].astype(F32)
        o_ref[:, s * pt:(s + 1) * pt, :] = (y_bt.reshape(bsz, pt, S5_WIDTH) * gate).astype(BF16)


def _s5(su, sz, consts, *, bsz, seq, tt):
    blk = pl.BlockSpec((bsz, tt, S5_WIDTH), lambda i: (0, i, 0))
    return pl.pallas_call(
        functools.partial(_s5_kernel, bsz=bsz, tt=tt),
        grid=(seq // tt,),
        in_specs=[blk, blk] + [_const_spec(a.shape) for a in consts],
        out_specs=blk,
        out_shape=jax.ShapeDtypeStruct((bsz, seq, S5_WIDTH), BF16),
        scratch_shapes=[pltpu.VMEM((bsz * tt, S5_LANES), F32), pltpu.VMEM((bsz * tt, S5_LANES), F32),
                        pltpu.VMEM((bsz, S5_LANES), F32), pltpu.VMEM((bsz, S5_LANES), F32)],
        compiler_params=pltpu.CompilerParams(dimension_semantics=("arbitrary",),
                                             vmem_limit_bytes=VMEM_LIMIT),
        name="s5",
    )(su, sz, *consts)


def _s5_params(lam_re, lam_im, log_dt, b_re, b_im, c_re, c_im, d_skip, w_glu, b_glu, bsz):
    n = lam_re.shape[0]
    dt = jnp.exp(log_dt.astype(F32))[..., None]
    mag = jnp.exp(lam_re * dt)
    abar_re = mag * jnp.cos(lam_im * dt)
    abar_im = mag * jnp.sin(lam_im * dt)
    den = lam_re * lam_re + lam_im * lam_im
    nr, ni = abar_re - 1.0, abar_im
    coef_re = (nr * lam_re + ni * lam_im) / den
    coef_im = (ni * lam_re - nr * lam_im) / den
    bbar_re = coef_re[..., None] * b_re - coef_im[..., None] * b_im
    bbar_im = coef_re[..., None] * b_im + coef_im[..., None] * b_re
    eye = jnp.eye(S5_GROUPS, dtype=F32)

    def in_map(bb):
        full = jnp.einsum('ngph,gk->nghkp', bb, eye).reshape(n, S5_WIDTH, S5_LANES)
        return jnp.stack([full[:, :S5_HALF_IN, :S5_HALF_ST], full[:, S5_HALF_IN:, S5_HALF_ST:]], axis=1).astype(BF16)

    def out_map(cc):
        full = jnp.einsum('nghp,gk->ngpkh', cc, eye).reshape(n, S5_LANES, S5_WIDTH)
        return jnp.stack([full[:, :S5_HALF_ST, :S5_HALF_IN], full[:, S5_HALF_ST:, S5_HALF_IN:]], axis=1).astype(BF16)

    bcast = lambda a: jnp.broadcast_to(a.reshape(n, 1, S5_LANES), (n, bsz, S5_LANES))
    r = jnp.arange(bsz * S5_PERM_T)
    perm = (r[None, :] == ((r % bsz) * S5_PERM_T + r // bsz)[:, None]).astype(BF16)
    per_layer = [in_map(bbar_re), in_map(bbar_im), out_map(c_re), out_map(c_im), bcast(abar_re), bcast(abar_im),
                 d_skip.reshape(n, 1, -1), w_glu.astype(BF16), b_glu.reshape(n, 1, -1)]
    return [[perm, perm.T] + [a[i] for a in per_layer] for i in range(n)]


def _odd_mix_kernel(ys_ref, ug_ref, vn_ref, h_ref, ws_ref, bs_ref, wout_ref, fg_ref, o_ref, y_ref,
                    *, tt, final_norm):
    y_ref[:, 0:S5_WIDTH] = ys_ref[...]
    for ci in range(tt // SG_CHUNK):
        rl = slice(ci * SG_CHUNK, (ci + 1) * SG_CHUNK)
        for hd in range(SG_HEADS):
            ll = slice(hd * SG_HD, (hd + 1) * SG_HD)
            sv = jnp.dot(ws_ref[hd], vn_ref[rl, ll], preferred_element_type=F32) + bs_ref[:, ll]
            y_ref[rl, S5_WIDTH + hd * SG_HD:S5_WIDTH + (hd + 1) * SG_HD] = (
                ug_ref[rl, ll].astype(F32) * sv).astype(BF16)
    out = h_ref[...] + jnp.dot(y_ref[...], wout_ref[...], preferred_element_type=F32)
    if final_norm:
        out = _rms_rows(out, fg_ref[...])
    o_ref[...] = out


def _odd_mix(ys, ug, vn, h, ws, bs, wout, fg, *, tt, final_norm):
    t = h.shape[0]
    row = lambda c: pl.BlockSpec((tt, c), lambda i: (i, 0))
    consts = [ws, bs, wout, fg]
    return pl.pallas_call(
        functools.partial(_odd_mix_kernel, tt=tt, final_norm=final_norm),
        grid=(t // tt,),
        in_specs=[row(S5_WIDTH), row(SG_WIDTH), row(SG_WIDTH), row(D_MODEL)]
                 + [_const_spec(a.shape) for a in consts],
        out_specs=row(D_MODEL),
        out_shape=jax.ShapeDtypeStruct(h.shape, F32),
        scratch_shapes=[pltpu.VMEM((tt, S5_WIDTH + SG_WIDTH), BF16)],
        compiler_params=pltpu.CompilerParams(dimension_semantics=("arbitrary",),
                                             vmem_limit_bytes=VMEM_LIMIT),
        name="odd_mix",
    )(ys, ug, vn, h, *consts)


def _odd_layer(h, in_consts, s5_consts, mix_consts, *, bsz, seq, final_norm):
    su, sz, ug, vn = _odd_in(h, *in_consts, tm=IN_TM)
    ys = _s5(su.reshape(bsz, seq, S5_WIDTH), sz.reshape(bsz, seq, S5_WIDTH), s5_consts, bsz=bsz, seq=seq, tt=S5_TT)
    return _odd_mix(ys.reshape(bsz * seq, S5_WIDTH), ug, vn, h, *mix_consts, tt=ODD_TT, final_norm=final_norm)


def kernel(x, norm_g, final_g, e_w_in, e_w_a2, e_b_a, e_gla_g, e_conv_w, e_conv_b, e_cln_g, e_cln_b, e_w_out, o_w_in, o_lam_re, o_lam_im, o_log_dt, o_b_re, o_b_im, o_c_re, o_c_im, o_d, o_w_glu, o_b_glu, o_sg_ln_g, o_sg_ln_b, o_w_s, o_b_s, o_w_out):
    bsz, seq, d = x.shape
    assert d == D_MODEL and bsz == SUBLANE and seq % max(EVEN_TT, ODD_TT, IN_TM) == 0
    depth = norm_g.shape[0]
    assert depth % 2 == 0
    n_odd = depth // 2

    even_consts = _even_params(norm_g[0::2], e_w_in, e_w_a2, e_b_a, e_gla_g, e_conv_w, e_conv_b, e_cln_g, e_cln_b,
                               e_w_out)
    s5_consts = _s5_params(o_lam_re, o_lam_im, o_log_dt, o_b_re, o_b_im, o_c_re, o_c_im, o_d, o_w_glu, o_b_glu, bsz)
    vec = lambda a: a.reshape(n_odd, 1, -1)
    o_g, o_w, o_lng, o_lnb = vec(norm_g[1::2]), o_w_in.astype(BF16), vec(o_sg_ln_g), vec(o_sg_ln_b)
    causal = jnp.tril(jnp.ones((SG_CHUNK, SG_CHUNK), dtype=bool))
    ws = jnp.where(causal, o_w_s, 0.0).astype(BF16)
    bs = jnp.repeat(jnp.swapaxes(o_b_s, 1, 2), SG_HD, axis=2)
    o_wout = o_w_out.astype(BF16)
    fg = final_g.reshape(1, -1)

    h = x.reshape(bsz * seq, d)
    for layer in range(depth):
        i = layer // 2
        if layer % 2 == 0:
            h = _even_layer(h, even_consts[i], bsz=bsz, seq=seq)
        else:
            h = _odd_layer(h, (o_g[i], o_w[i], o_lng[i], o_lnb[i]), s5_consts[i], (ws[i], bs[i], o_wout[i], fg),
                           bsz=bsz, seq=seq, final_norm=layer == depth - 1)
    return h.reshape(bsz, seq, d)
```

```python
import functools
import math

import jax
import jax.numpy as jnp
from jax import lax
from jax.experimental import pallas as pl
from jax.experimental.pallas import tpu as pltpu

F32 = jnp.float32
BF16 = jnp.bfloat16

EPS = 1e-6
D_MODEL = 1024

GLA_HEADS = 4
GLA_DK = D_MODEL // 2
GLA_DV = D_MODEL
GLA_HK = GLA_DK // GLA_HEADS
GLA_HV = GLA_DV // GLA_HEADS
GLA_RANK = 16
GLA_TAU = 16.0
GLA_CHUNK = 64

CONV_WIDTH = D_MODEL
CONV_K = 31
CONV_HALO = 32

S5_WIDTH = D_MODEL // 2
S5_GROUP = 16
S5_GROUPS = S5_WIDTH // S5_GROUP
S5_STATE = 64
S5_LANES = S5_GROUPS * S5_STATE
S5_HALF_IN = S5_WIDTH // 2
S5_HALF_ST = S5_LANES // 2

SG_WIDTH = D_MODEL
SG_HEADS = 8
SG_HD = SG_WIDTH // SG_HEADS
SG_CHUNK = 128

LANE = 128
SUBLANE = 8
A_LOW_PAD = LANE
VMEM_LIMIT = 56 * 1024 * 1024
S5_TT = 64
ODD_TT = 1024


def _silu(x):
    return x * jax.nn.sigmoid(x)


def _rms_rows(x, g):
    return x * lax.rsqrt(jnp.mean(x * x, axis=-1, keepdims=True) + EPS) * g


def _layernorm_rows(x, g, b):
    mu = jnp.mean(x, axis=-1, keepdims=True)
    xc = x - mu
    var = jnp.mean(xc * xc, axis=-1, keepdims=True)
    return xc * lax.rsqrt(var + EPS) * g + b


def _log_sigmoid(x):
    return jnp.minimum(x, 0.0) - jnp.log1p(jnp.exp(-jnp.abs(x)))


def _gelu_tanh(x):
    c = math.sqrt(2.0 / math.pi)
    return 0.5 * x * (1.0 + jnp.tanh(c * (x + 0.044715 * (x * x * x))))


def _const_spec(shape):
    nd = len(shape)
    return pl.BlockSpec(shape, lambda *_: (0,) * nd, pipeline_mode=pl.Buffered(1))


CONV_ROWS = 64
CONV_SEG = 32
EVEN_TT = CONV_SEG * SUBLANE
CONV_EXT_ROWS = (CONV_HALO + CONV_SEG) * SUBLANE
EVEN_GROUPS = EVEN_TT // CONV_ROWS
E_A0 = 2 * GLA_DK + 2 * GLA_DV
E_CV0 = E_A0 + GLA_RANK
E_CZ0 = E_CV0 + 2 * CONV_WIDTH
P_QK, P_V, P_Z, P_CZ = range(EVEN_GROUPS)


def _conv_row_maps():
    te, seg = jnp.divmod(jnp.arange(CONV_EXT_ROWS), SUBLANE)
    src = seg * CONV_SEG + te - CONV_HALO
    sel_cur = (src[:, None] == jnp.arange(EVEN_TT)[None, :]).astype(BF16)
    sel_prev = ((src + CONV_HALO)[:, None] == jnp.arange(CONV_HALO)[None, :]).astype(BF16)
    ts, sg = jnp.divmod(jnp.arange(EVEN_TT), SUBLANE)
    unsel = (jnp.arange(EVEN_TT)[:, None] == (sg * CONV_SEG + ts)[None, :]).astype(BF16)
    return sel_cur, sel_prev, unsel


def _even_kernel(h_ref, g_ref, wc_ref, wl_ref, wm_ref, wa2_ref, ba_ref, glag_ref, cw_ref, cb_ref, lng_ref,
                 lnb_ref, wout_ref, selc_ref, selp_ref, unsel_ref,
                 o_ref, st_ref, y_ref, tail_ref, ext_ref, conv_ref, xb_ref, xn_ref, p_ref, la_ref, *, tt):
    @pl.when(pl.program_id(1) == 0)
    def _():
        st_ref[...] = jnp.zeros_like(st_ref)
        tail_ref[...] = jnp.zeros_like(tail_ref)

    xn = _rms_rows(h_ref[...], g_ref[...]).astype(BF16)
    xn_ref[...] = xn

    c_val = jnp.dot(xn, wc_ref[:, 0:CONV_WIDTH], preferred_element_type=F32)
    c_gate = jnp.dot(xn, wc_ref[:, CONV_WIDTH:2 * CONV_WIDTH], preferred_element_type=F32)
    u = (c_val * jax.nn.sigmoid(c_gate)).astype(BF16)
    ext_ref[...] = (jnp.dot(selc_ref[...], u, preferred_element_type=F32)
                    + jnp.dot(selp_ref[...], tail_ref[...], preferred_element_type=F32))
    tail_ref[...] = u[tt - CONV_HALO:tt, :]

    a_low = jnp.dot(xn, wl_ref[...], preferred_element_type=F32).astype(BF16)
    logit = jnp.dot(a_low, wa2_ref[...], preferred_element_type=F32) + ba_ref[...]
    la_ref[...] = _log_sigmoid(logit) * (1.0 / GLA_TAU)

    off0 = CONV_HALO - (CONV_K - 1)
    rsub = CONV_ROWS // SUBLANE

    def conv_block(rb, carry):
        p_ref[rb] = jnp.dot(xn_ref[...], wm_ref[rb], preferred_element_type=F32).astype(BF16)
        r0 = pl.multiple_of(rb * CONV_ROWS, CONV_ROWS)
        for lb in range(CONV_WIDTH // LANE):
            ll = slice(lb * LANE, (lb + 1) * LANE)
            acc = [jnp.zeros((SUBLANE, LANE), F32) for _ in range(rsub)]
            for e in range(CONV_K + rsub - 1):
                x = ext_ref[pl.ds(r0 + (off0 + e) * SUBLANE, SUBLANE), ll]
                for j in range(rsub):
                    kk = e - j
                    if 0 <= kk < CONV_K:
                        acc[j] = acc[j] + x * cw_ref[kk * SUBLANE:(kk + 1) * SUBLANE, ll]
            for j in range(rsub):
                conv_ref[pl.ds(r0 + j * SUBLANE, SUBLANE), ll] = acc[j]
        x = conv_ref[pl.ds(r0, CONV_ROWS), :] + cb_ref[...]
        xb_ref[pl.ds(r0, CONV_ROWS), :] = _silu(_layernorm_rows(x, lng_ref[...], lnb_ref[...])).astype(BF16)
        return carry

    lax.fori_loop(0, EVEN_GROUPS, conv_block, 0)
    y_conv = jnp.dot(unsel_ref[...], xb_ref[...], preferred_element_type=F32)
    y_ref[:, GLA_DV:GLA_DV + CONV_WIDTH] = (y_conv * _silu(p_ref[P_CZ].astype(F32))).astype(BF16)

    c = GLA_CHUNK
    nc = tt // c
    rows = lax.broadcasted_iota(jnp.int32, (tt, tt), 0)
    cols = lax.broadcasted_iota(jnp.int32, (tt, tt), 1)
    causal = (rows >= cols) & ((rows // c) == (cols // c))
    tri = causal.astype(BF16)
    scale = GLA_HK ** -0.5
    nt_dims = (((1,), (1,)), ((), ()))
    tn_dims = (((0,), (0,)), ((), ()))

    g = la_ref[...]
    g_hi = g.astype(BF16)
    r1 = g - g_hi.astype(F32)
    g_mid = r1.astype(BF16)
    g_lo = (r1 - g_mid.astype(F32)).astype(BF16)
    b = (jnp.dot(tri, g_hi, preferred_element_type=F32) + jnp.dot(tri, g_mid, preferred_element_type=F32)
         + jnp.dot(tri, g_lo, preferred_element_type=F32))
    b_last = [b[ci * c + c - 1:ci * c + c, :] for ci in range(nc)]
    b_last_rows = jnp.concatenate([jnp.broadcast_to(bl, (c, GLA_DK)) for bl in b_last], axis=0)
    q = p_ref[P_QK, :, 0:GLA_DK].astype(F32)
    k = p_ref[P_QK, :, GLA_DK:2 * GLA_DK].astype(F32)
    qf = (q * jnp.exp(b) * scale).astype(BF16)
    k_intra = (k * jnp.exp(-b)).astype(BF16)
    k_state = (k * jnp.exp(b_last_rows - b)).astype(BF16)
    decay = [jnp.exp(bl) for bl in b_last]

    for hd in range(GLA_HEADS):
        kl = slice(hd * GLA_HK, (hd + 1) * GLA_HK)
        vl = slice(hd * GLA_HV, (hd + 1) * GLA_HV)
        v = p_ref[P_V, :, vl]
        att = lax.dot_general(qf[:, kl], k_intra[:, kl], nt_dims, preferred_element_type=F32)
        att = jnp.where(causal, att, 0.0).astype(BF16)
        o_intra = jnp.dot(att, v, preferred_element_type=F32)
        s_t = st_ref[hd]
        o_inter = []
        for ci in range(nc):
            rl = slice(ci * c, (ci + 1) * c)
            o_inter.append(lax.dot_general(qf[rl, kl], s_t.astype(BF16), nt_dims, preferred_element_type=F32))
            kv_t = lax.dot_general(v[rl, :], k_state[rl, kl], tn_dims, preferred_element_type=F32)
            s_t = s_t * decay[ci][:, kl] + kv_t
        st_ref[hd] = s_t
        o = o_intra + jnp.concatenate(o_inter, axis=0)
        o = o * lax.rsqrt(jnp.mean(o * o, axis=-1, keepdims=True) + EPS) * glag_ref[:, vl]
        y_ref[:, vl] = (o * _silu(p_ref[P_Z, :, vl].astype(F32))).astype(BF16)

    o_ref[...] = h_ref[...] + jnp.dot(y_ref[...], wout_ref[...], preferred_element_type=F32)


def _even_layer(h, consts, *, bsz, seq):
    tt = EVEN_TT
    nt = seq // tt
    row = pl.BlockSpec((tt, D_MODEL), lambda b, i: (b * nt + i, 0))
    return pl.pallas_call(
        functools.partial(_even_kernel, tt=tt),
        grid=(bsz, nt),
        in_specs=[row] + [_const_spec(a.shape) for a in consts],
        out_specs=row,
        out_shape=jax.ShapeDtypeStruct(h.shape, F32),
        scratch_shapes=[pltpu.VMEM((GLA_HEADS, GLA_HV, GLA_HK), F32),
                        pltpu.VMEM((tt, GLA_DV + CONV_WIDTH), BF16),
                        pltpu.VMEM((CONV_HALO, CONV_WIDTH), BF16),
                        pltpu.VMEM((CONV_EXT_ROWS, CONV_WIDTH), F32),
                        pltpu.VMEM((tt, CONV_WIDTH), F32),
                        pltpu.VMEM((tt, CONV_WIDTH), BF16),
                        pltpu.VMEM((tt, D_MODEL), BF16),
                        pltpu.VMEM((EVEN_GROUPS, tt, D_MODEL), BF16),
                        pltpu.VMEM((tt, GLA_DK), F32)],
        compiler_params=pltpu.CompilerParams(dimension_semantics=("arbitrary", "arbitrary"),
                                             vmem_limit_bytes=VMEM_LIMIT),
        name="even_layer",
    )(h, *consts)


def _even_params(norm_g, w_in, w_a2, b_a, gla_g, conv_w, conv_b, cln_g, cln_b, w_out):
    n = w_in.shape[0]
    assert EVEN_GROUPS == 4
    wm = jnp.stack([w_in[:, :, 0:2 * GLA_DK], w_in[:, :, 2 * GLA_DK:2 * GLA_DK + GLA_DV],
                    w_in[:, :, 2 * GLA_DK + GLA_DV:E_A0], w_in[:, :, E_CZ0:]], axis=1).astype(BF16)
    wl = jnp.pad(w_in[:, :, E_A0:E_CV0], ((0, 0), (0, 0), (0, A_LOW_PAD - GLA_RANK))).astype(BF16)
    wc = w_in[:, :, E_CV0:E_CZ0].astype(BF16)
    wa2 = jnp.pad(w_a2, ((0, 0), (0, A_LOW_PAD - GLA_RANK), (0, 0))).astype(BF16)
    cw = jnp.repeat(conv_w, SUBLANE, axis=1)
    wout = w_out.astype(BF16)
    maps = _conv_row_maps()
    vec = lambda a: a.reshape(n, 1, -1)
    per_layer = [vec(norm_g), wc, wl, wm, wa2, vec(b_a), vec(gla_g), cw, vec(conv_b), vec(cln_g), vec(cln_b), wout]
    return [[a[i] for a in per_layer] + list(maps) for i in range(n)]


O_GU0 = 2 * S5_WIDTH
S5_SCAN_LANES = 512
S5_PERM_T = 32


def _s5_kernel(h_ref, g_ref, w_ref, perm_ref, permt_ref, bre_ref, bim_ref, cre_ref, cim_ref, are_ref, aim_ref,
               d_ref, wglu_ref, bglu_ref, o_ref, u_ref, sz_ref, xre_ref, xim_ref, sre_ref, sim_ref, *, bsz, tt):
    @pl.when(pl.program_id(0) == 0)
    def _():
        sre_ref[...] = jnp.zeros_like(sre_ref)
        sim_ref[...] = jnp.zeros_like(sim_ref)

    xn = _rms_rows(h_ref[...].reshape(bsz * tt, D_MODEL), g_ref[...]).astype(BF16)
    u_ref[...] = jnp.dot(xn, w_ref[:, 0:S5_WIDTH], preferred_element_type=F32).astype(BF16).reshape(
        bsz, tt, S5_WIDTH)
    sz_ref[...] = _silu(jnp.dot(xn, w_ref[:, S5_WIDTH:2 * S5_WIDTH], preferred_element_type=F32)).astype(
        BF16).reshape(bsz, tt, S5_WIDTH)

    pt = S5_PERM_T
    prow = bsz * pt
    u = jnp.concatenate(
        [jnp.dot(perm_ref[...], u_ref[:, s * pt:(s + 1) * pt, :].reshape(prow, S5_WIDTH),
                 preferred_element_type=F32).astype(BF16) for s in range(tt // pt)], axis=0)
    for hf in range(2):
        ul = u[:, hf * S5_HALF_IN:(hf + 1) * S5_HALF_IN]
        sl = slice(hf * S5_HALF_ST, (hf + 1) * S5_HALF_ST)
        xre_ref[:, sl] = jnp.dot(ul, bre_ref[hf], preferred_element_type=F32)
        xim_ref[:, sl] = jnp.dot(ul, bim_ref[hf], preferred_element_type=F32)

    for cb in range(S5_LANES // S5_SCAN_LANES):
        ll = slice(cb * S5_SCAN_LANES, (cb + 1) * S5_SCAN_LANES)
        a_re = are_ref[:, ll]
        a_im = aim_ref[:, ll]
        x_re = sre_ref[:, ll]
        x_im = sim_ref[:, ll]
        for t in range(tt):
            idx = slice(t * bsz, (t + 1) * bsz)
            x_re, x_im = (a_re * x_re - a_im * x_im + xre_ref[idx, ll],
                          a_re * x_im + a_im * x_re + xim_ref[idx, ll])
            xre_ref[idx, ll] = x_re
            xim_ref[idx, ll] = x_im
        sre_ref[:, ll] = x_re
        sim_ref[:, ll] = x_im

    ys = []
    for hf in range(2):
        sl = slice(hf * S5_HALF_ST, (hf + 1) * S5_HALF_ST)
        y = jnp.dot(xre_ref[:, sl].astype(BF16), cre_ref[hf], preferred_element_type=F32)
        y = y - jnp.dot(xim_ref[:, sl].astype(BF16), cim_ref[hf], preferred_element_type=F32)
        ys.append(y)
    y = jnp.concatenate(ys, axis=-1) + d_ref[...] * u.astype(F32)
    y = _gelu_tanh(y)
    y = y * jax.nn.sigmoid(jnp.dot(y.astype(BF16), wglu_ref[...], preferred_element_type=F32) + bglu_ref[...])
    y = y.astype(BF16)
    for s in range(tt // pt):
        y_bt = jnp.dot(permt_ref[...], y[s * prow:(s + 1) * prow, :], preferred_element_type=F32)
        gate = sz_ref[:, s * pt:(s + 1) * pt, :].astype(F32)
        o_ref[:, s * pt:(s + 1) * pt, :] = (y_bt.reshape(bsz, pt, S5_WIDTH) * gate).astype(BF16)


def _s5(h3, consts, *, bsz, seq, tt):
    blk = lambda c: pl.BlockSpec((bsz, tt, c), lambda i: (0, i, 0))
    return pl.pallas_call(
        functools.partial(_s5_kernel, bsz=bsz, tt=tt),
        grid=(seq // tt,),
        in_specs=[blk(D_MODEL)] + [_const_spec(a.shape) for a in consts],
        out_specs=blk(S5_WIDTH),
        out_shape=jax.ShapeDtypeStruct((bsz, seq, S5_WIDTH), BF16),
        scratch_shapes=[pltpu.VMEM((bsz, tt, S5_WIDTH), BF16), pltpu.VMEM((bsz, tt, S5_WIDTH), BF16),
                        pltpu.VMEM((bsz * tt, S5_LANES), F32), pltpu.VMEM((bsz * tt, S5_LANES), F32),
                        pltpu.VMEM((bsz, S5_LANES), F32), pltpu.VMEM((bsz, S5_LANES), F32)],
        compiler_params=pltpu.CompilerParams(dimension_semantics=("arbitrary",),
                                             vmem_limit_bytes=VMEM_LIMIT),
        name="s5",
    )(h3, *consts)


def _s5_params(lam_re, lam_im, log_dt, b_re, b_im, c_re, c_im, d_skip, w_glu, b_glu, bsz):
    n = lam_re.shape[0]
    dt = jnp.exp(log_dt.astype(F32))[..., None]
    mag = jnp.exp(lam_re * dt)
    abar_re = mag * jnp.cos(lam_im * dt)
    abar_im = mag * jnp.sin(lam_im * dt)
    den = lam_re * lam_re + lam_im * lam_im
    nr, ni = abar_re - 1.0, abar_im
    coef_re = (nr * lam_re + ni * lam_im) / den
    coef_im = (ni * lam_re - nr * lam_im) / den
    bbar_re = coef_re[..., None] * b_re - coef_im[..., None] * b_im
    bbar_im = coef_re[..., None] * b_im + coef_im[..., None] * b_re
    eye = jnp.eye(S5_GROUPS, dtype=F32)

    def in_map(bb):
        full = jnp.einsum('ngph,gk->nghkp', bb, eye).reshape(n, S5_WIDTH, S5_LANES)
        return jnp.stack([full[:, :S5_HALF_IN, :S5_HALF_ST], full[:, S5_HALF_IN:, S5_HALF_ST:]], axis=1).astype(BF16)

    def out_map(cc):
        full = jnp.einsum('nghp,gk->ngpkh', cc, eye).reshape(n, S5_LANES, S5_WIDTH)
        return jnp.stack([full[:, :S5_HALF_ST, :S5_HALF_IN], full[:, S5_HALF_ST:, S5_HALF_IN:]], axis=1).astype(BF16)

    bcast = lambda a: jnp.broadcast_to(a.reshape(n, 1, S5_LANES), (n, bsz, S5_LANES))
    r = jnp.arange(bsz * S5_PERM_T)
    perm = (r[None, :] == ((r % bsz) * S5_PERM_T + r // bsz)[:, None]).astype(BF16)
    per_layer = [in_map(bbar_re), in_map(bbar_im), out_map(c_re), out_map(c_im), bcast(abar_re), bcast(abar_im),
                 d_skip.reshape(n, 1, -1), w_glu.astype(BF16), b_glu.reshape(n, 1, -1)]
    return [[perm, perm.T] + [a[i] for a in per_layer] for i in range(n)]


def _odd_mix_kernel(ys_ref, h_ref, g_ref, w_ref, lng_ref, lnb_ref, ws_ref, bs_ref, wout_ref, fg_ref,
                    o_ref, y_ref, ug_ref, vn_ref, *, tt, final_norm):
    xn = _rms_rows(h_ref[...], g_ref[...]).astype(BF16)

    def proj(j):
        return jnp.dot(xn, w_ref[:, j * SG_WIDTH:(j + 1) * SG_WIDTH], preferred_element_type=F32)

    ug_ref[...] = (proj(0) * _silu(proj(2))).astype(BF16)
    vn_ref[...] = _layernorm_rows(proj(1), lng_ref[...], lnb_ref[...]).astype(BF16)

    y_ref[:, 0:S5_WIDTH] = ys_ref[...]
    for ci in range(tt // SG_CHUNK):
        rl = slice(ci * SG_CHUNK, (ci + 1) * SG_CHUNK)
        for hd in range(SG_HEADS):
            ll = slice(hd * SG_HD, (hd + 1) * SG_HD)
            sv = jnp.dot(ws_ref[hd], vn_ref[rl, ll], preferred_element_type=F32) + bs_ref[:, ll]
            y_ref[rl, S5_WIDTH + hd * SG_HD:S5_WIDTH + (hd + 1) * SG_HD] = (
                ug_ref[rl, ll].astype(F32) * sv).astype(BF16)
    out = h_ref[...] + jnp.dot(y_ref[...], wout_ref[...], preferred_element_type=F32)
    if final_norm:
        out = _rms_rows(out, fg_ref[...])
    o_ref[...] = out


def _odd_mix(ys, h, consts, *, tt, final_norm):
    t = h.shape[0]
    row = lambda c: pl.BlockSpec((tt, c), lambda i: (i, 0))
    return pl.pallas_call(
        functools.partial(_odd_mix_kernel, tt=tt, final_norm=final_norm),
        grid=(t // tt,),
        in_specs=[row(S5_WIDTH), row(D_MODEL)] + [_const_spec(a.shape) for a in consts],
        out_specs=row(D_MODEL),
        out_shape=jax.ShapeDtypeStruct(h.shape, F32),
        scratch_shapes=[pltpu.VMEM((tt, S5_WIDTH + SG_WIDTH), BF16), pltpu.VMEM((tt, SG_WIDTH), BF16),
                        pltpu.VMEM((tt, SG_WIDTH), BF16)],
        compiler_params=pltpu.CompilerParams(dimension_semantics=("arbitrary",),
                                             vmem_limit_bytes=VMEM_LIMIT),
        name="odd_mix",
    )(ys, h, *consts)


def _odd_layer(h, s5_consts, mix_consts, *, bsz, seq, final_norm):
    ys = _s5(h.reshape(bsz, seq, D_MODEL), s5_consts, bsz=bsz, seq=seq, tt=S5_TT)
    return _odd_mix(ys.reshape(bsz * seq, S5_WIDTH), h, mix_consts, tt=ODD_TT, final_norm=final_norm)


def kernel(x, norm_g, final_g, e_w_in, e_w_a2, e_b_a, e_gla_g, e_conv_w, e_conv_b, e_cln_g, e_cln_b, e_w_out, o_w_in, o_lam_re, o_lam_im, o_log_dt, o_b_re, o_b_im, o_c_re, o_c_im, o_d, o_w_glu, o_b_glu, o_sg_ln_g, o_sg_ln_b, o_w_s, o_b_s, o_w_out):
    bsz, seq, d = x.shape
    assert d == D_MODEL and bsz == SUBLANE and seq % max(EVEN_TT, ODD_TT) == 0
    depth = norm_g.shape[0]
    assert depth % 2 == 0
    n_odd = depth // 2

    even_consts = _even_params(norm_g[0::2], e_w_in, e_w_a2, e_b_a, e_gla_g, e_conv_w, e_conv_b, e_cln_g, e_cln_b,
                               e_w_out)
    s5_consts = _s5_params(o_lam_re, o_lam_im, o_log_dt, o_b_re, o_b_im, o_c_re, o_c_im, o_d, o_w_glu, o_b_glu, bsz)
    vec = lambda a: a.reshape(n_odd, 1, -1)
    o_g, o_lng, o_lnb = vec(norm_g[1::2]), vec(o_sg_ln_g), vec(o_sg_ln_b)
    o_w_s5 = o_w_in[:, :, :O_GU0].astype(BF16)
    o_w_sg = o_w_in[:, :, O_GU0:].astype(BF16)
    causal = jnp.tril(jnp.ones((SG_CHUNK, SG_CHUNK), dtype=bool))
    ws = jnp.where(causal, o_w_s, 0.0).astype(BF16)
    bs = jnp.repeat(jnp.swapaxes(o_b_s, 1, 2), SG_HD, axis=2)
    o_wout = o_w_out.astype(BF16)
    fg = final_g.reshape(1, -1)

    h = x.reshape(bsz * seq, d)
    for layer in range(depth):
        i = layer // 2
        if layer % 2 == 0:
            h = _even_layer(h, even_consts[i], bsz=bsz, seq=seq)
        else:
            h = _odd_layer(h, [o_g[i], o_w_s5[i]] + s5_consts[i],
                           [o_g[i], o_w_sg[i], o_lng[i], o_lnb[i], ws[i], bs[i], o_wout[i], fg],
                           bsz=bsz, seq=seq, final_norm=layer == depth - 1)
    return h.reshape(bsz, seq, d)
```

```python
import functools
import math

import jax
import jax.numpy as jnp
from jax import lax
from jax.experimental import pallas as pl
from jax.experimental.pallas import tpu as pltpu

F32 = jnp.float32
BF16 = jnp.bfloat16

EPS = 1e-6
D_MODEL = 1024

GLA_HEADS = 4
GLA_DK = D_MODEL // 2
GLA_DV = D_MODEL
GLA_HK = GLA_DK // GLA_HEADS
GLA_HV = GLA_DV // GLA_HEADS
GLA_RANK = 16
GLA_TAU = 16.0
GLA_CHUNK = 64

CONV_WIDTH = D_MODEL
CONV_K = 31
CONV_HALO = 32

S5_WIDTH = D_MODEL // 2
S5_GROUP = 16
S5_GROUPS = S5_WIDTH // S5_GROUP
S5_STATE = 64
S5_LANES = S5_GROUPS * S5_STATE
S5_HALF_IN = S5_WIDTH // 2
S5_HALF_ST = S5_LANES // 2

SG_WIDTH = D_MODEL
SG_HEADS = 8
SG_HD = SG_WIDTH // SG_HEADS
SG_CHUNK = 128

LANE = 128
SUBLANE = 8
A_LOW_PAD = LANE
VMEM_LIMIT = 56 * 1024 * 1024
S5_TT = 64
ODD_TT = 1024


def _silu(x):
    return x * jax.nn.sigmoid(x)


def _rms_rows(x, g):
    return x * lax.rsqrt(jnp.mean(x * x, axis=-1, keepdims=True) + EPS) * g


def _layernorm_rows(x, g, b):
    mu = jnp.mean(x, axis=-1, keepdims=True)
    xc = x - mu
    var = jnp.mean(xc * xc, axis=-1, keepdims=True)
    return xc * lax.rsqrt(var + EPS) * g + b


def _log_sigmoid(x):
    return jnp.minimum(x, 0.0) - jnp.log1p(jnp.exp(-jnp.abs(x)))


def _gelu_tanh(x):
    c = math.sqrt(2.0 / math.pi)
    return 0.5 * x * (1.0 + jnp.tanh(c * (x + 0.044715 * (x * x * x))))


def _layer_spec(a, i):
    nd = a.ndim - 1
    return pl.BlockSpec((None,) + a.shape[1:], lambda *_: (i,) + (0,) * nd, pipeline_mode=pl.Buffered(1))


def _layer_specs(consts, layer):
    stacked, shared = consts
    return [_layer_spec(a, layer) for a in stacked] + [_layer_spec(a, 0) for a in shared]


CONV_ROWS = 64
CONV_SEG = 32
EVEN_TT = CONV_SEG * SUBLANE
CONV_EXT_ROWS = (CONV_HALO + CONV_SEG) * SUBLANE
EVEN_SUB = 4
EVEN_GROUPS = EVEN_TT // CONV_ROWS
E_A0 = 2 * GLA_DK + 2 * GLA_DV
E_CV0 = E_A0 + GLA_RANK
E_CZ0 = E_CV0 + 2 * CONV_WIDTH
P_QK, P_V, P_Z, P_CZ = range(EVEN_GROUPS)


def _conv_row_maps():
    te, seg = jnp.divmod(jnp.arange(CONV_EXT_ROWS), SUBLANE)
    src = seg * CONV_SEG + te - CONV_HALO
    sel_cur = (src[:, None] == jnp.arange(EVEN_TT)[None, :]).astype(BF16)
    sel_prev = ((src + CONV_HALO)[:, None] == jnp.arange(CONV_HALO)[None, :]).astype(BF16)
    ts, sg = jnp.divmod(jnp.arange(EVEN_TT), SUBLANE)
    unsel = (jnp.arange(EVEN_TT)[:, None] == (sg * CONV_SEG + ts)[None, :]).astype(BF16)
    return sel_cur, sel_prev, unsel


def _even_kernel(h_ref, *refs, tt, nsub):
    n_in = 15
    o_ref, st_ref, tail_ref = refs[n_in], refs[n_in + 1], refs[n_in + 3]

    @pl.when(pl.program_id(1) == 0)
    def _():
        st_ref[...] = jnp.zeros_like(st_ref)
        tail_ref[...] = jnp.zeros_like(tail_ref)

    def tile(sub, carry):
        rows = pl.ds(pl.multiple_of(sub * tt, tt), tt)
        _even_tile(h_ref.at[rows], *refs[:n_in], o_ref.at[rows], *refs[n_in + 1:], tt=tt)
        return carry

    lax.fori_loop(0, nsub, tile, 0)


def _even_tile(h_ref, g_ref, wc_ref, wl_ref, wm_ref, wa2_ref, ba_ref, glag_ref, cw_ref, cb_ref, lng_ref,
               lnb_ref, wout_ref, selc_ref, selp_ref, unsel_ref,
               o_ref, st_ref, y_ref, tail_ref, ext_ref, conv_ref, xb_ref, xn_ref, p_ref, la_ref, *, tt):
    xn = _rms_rows(h_ref[...], g_ref[...]).astype(BF16)
    xn_ref[...] = xn

    c_val = jnp.dot(xn, wc_ref[:, 0:CONV_WIDTH], preferred_element_type=F32)
    c_gate = jnp.dot(xn, wc_ref[:, CONV_WIDTH:2 * CONV_WIDTH], preferred_element_type=F32)
    u = (c_val * jax.nn.sigmoid(c_gate)).astype(BF16)
    ext_ref[...] = (jnp.dot(selc_ref[...], u, preferred_element_type=F32)
                    + jnp.dot(selp_ref[...], tail_ref[...], preferred_element_type=F32))
    tail_ref[...] = u[tt - CONV_HALO:tt, :]

    a_low = jnp.dot(xn, wl_ref[...], preferred_element_type=F32).astype(BF16)
    logit = jnp.dot(a_low, wa2_ref[...], preferred_element_type=F32) + ba_ref[...]
    la_ref[...] = _log_sigmoid(logit) * (1.0 / GLA_TAU)

    off0 = CONV_HALO - (CONV_K - 1)
    rsub = CONV_ROWS // SUBLANE

    def conv_block(rb, carry):
        p_ref[rb] = jnp.dot(xn_ref[...], wm_ref[rb], preferred_element_type=F32).astype(BF16)
        r0 = pl.multiple_of(rb * CONV_ROWS, CONV_ROWS)
        for lb in range(CONV_WIDTH // LANE):
            ll = slice(lb * LANE, (lb + 1) * LANE)
            acc = [jnp.zeros((SUBLANE, LANE), F32) for _ in range(rsub)]
            for e in range(CONV_K + rsub - 1):
                x = ext_ref[pl.ds(r0 + (off0 + e) * SUBLANE, SUBLANE), ll]
                for j in range(rsub):
                    kk = e - j
                    if 0 <= kk < CONV_K:
                        acc[j] = acc[j] + x * cw_ref[kk * SUBLANE:(kk + 1) * SUBLANE, ll]
            for j in range(rsub):
                conv_ref[pl.ds(r0 + j * SUBLANE, SUBLANE), ll] = acc[j]
        x = conv_ref[pl.ds(r0, CONV_ROWS), :] + cb_ref[...]
        xb_ref[pl.ds(r0, CONV_ROWS), :] = _silu(_layernorm_rows(x, lng_ref[...], lnb_ref[...])).astype(BF16)
        return carry

    lax.fori_loop(0, EVEN_GROUPS, conv_block, 0)
    y_conv = jnp.dot(unsel_ref[...], xb_ref[...], preferred_element_type=F32)
    y_ref[:, GLA_DV:GLA_DV + CONV_WIDTH] = (y_conv * _silu(p_ref[P_CZ].astype(F32))).astype(BF16)

    c = GLA_CHUNK
    nc = tt // c
    rows = lax.broadcasted_iota(jnp.int32, (tt, tt), 0)
    cols = lax.broadcasted_iota(jnp.int32, (tt, tt), 1)
    causal = (rows >= cols) & ((rows // c) == (cols // c))
    tri = causal.astype(BF16)
    scale = GLA_HK ** -0.5
    nt_dims = (((1,), (1,)), ((), ()))
    tn_dims = (((0,), (0,)), ((), ()))

    g = la_ref[...]
    g_hi = g.astype(BF16)
    r1 = g - g_hi.astype(F32)
    g_mid = r1.astype(BF16)
    g_lo = (r1 - g_mid.astype(F32)).astype(BF16)
    b = (jnp.dot(tri, g_hi, preferred_element_type=F32) + jnp.dot(tri, g_mid, preferred_element_type=F32)
         + jnp.dot(tri, g_lo, preferred_element_type=F32))
    b_last = [b[ci * c + c - 1:ci * c + c, :] for ci in range(nc)]
    b_last_rows = jnp.concatenate([jnp.broadcast_to(bl, (c, GLA_DK)) for bl in b_last], axis=0)
    q = p_ref[P_QK, :, 0:GLA_DK].astype(F32)
    k = p_ref[P_QK, :, GLA_DK:2 * GLA_DK].astype(F32)
    qf = (q * jnp.exp(b) * scale).astype(BF16)
    k_intra = (k * jnp.exp(-b)).astype(BF16)
    k_state = (k * jnp.exp(b_last_rows - b)).astype(BF16)
    decay = [jnp.exp(bl) for bl in b_last]

    for hd in range(GLA_HEADS):
        kl = slice(hd * GLA_HK, (hd + 1) * GLA_HK)
        vl = slice(hd * GLA_HV, (hd + 1) * GLA_HV)
        v = p_ref[P_V, :, vl]
        att = lax.dot_general(qf[:, kl], k_intra[:, kl], nt_dims, preferred_element_type=F32)
        att = jnp.where(causal, att, 0.0).astype(BF16)
        o_intra = jnp.dot(att, v, preferred_element_type=F32)
        s_t = st_ref[hd]
        o_inter = []
        for ci in range(nc):
            rl = slice(ci * c, (ci + 1) * c)
            o_inter.append(lax.dot_general(qf[rl, kl], s_t.astype(BF16), nt_dims, preferred_element_type=F32))
            kv_t = lax.dot_general(v[rl, :], k_state[rl, kl], tn_dims, preferred_element_type=F32)
            s_t = s_t * decay[ci][:, kl] + kv_t
        st_ref[hd] = s_t
        o = o_intra + jnp.concatenate(o_inter, axis=0)
        o = o * lax.rsqrt(jnp.mean(o * o, axis=-1, keepdims=True) + EPS) * glag_ref[:, vl]
        y_ref[:, vl] = (o * _silu(p_ref[P_Z, :, vl].astype(F32))).astype(BF16)

    o_ref[...] = h_ref[...] + jnp.dot(y_ref[...], wout_ref[...], preferred_element_type=F32)


def _even_layer(h, consts, layer, *, bsz, seq):
    tt = EVEN_TT
    row = pl.BlockSpec((None, tt * EVEN_SUB, D_MODEL), lambda b, i: (b, i, 0))
    return pl.pallas_call(
        functools.partial(_even_kernel, tt=tt, nsub=EVEN_SUB),
        grid=(bsz, seq // (tt * EVEN_SUB)),
        in_specs=[row] + _layer_specs(consts, layer),
        out_specs=row,
        out_shape=jax.ShapeDtypeStruct(h.shape, F32),
        scratch_shapes=[pltpu.VMEM((GLA_HEADS, GLA_HV, GLA_HK), F32),
                        pltpu.VMEM((tt, GLA_DV + CONV_WIDTH), BF16),
                        pltpu.VMEM((CONV_HALO, CONV_WIDTH), BF16),
                        pltpu.VMEM((CONV_EXT_ROWS, CONV_WIDTH), F32),
                        pltpu.VMEM((tt, CONV_WIDTH), F32),
                        pltpu.VMEM((tt, CONV_WIDTH), BF16),
                        pltpu.VMEM((tt, D_MODEL), BF16),
                        pltpu.VMEM((EVEN_GROUPS, tt, D_MODEL), BF16),
                        pltpu.VMEM((tt, GLA_DK), F32)],
        compiler_params=pltpu.CompilerParams(dimension_semantics=("arbitrary", "arbitrary"),
                                             vmem_limit_bytes=VMEM_LIMIT),
        name="even_layer",
    )(h, *consts[0], *consts[1])


def _even_params(norm_g, w_in, w_a2, b_a, gla_g, conv_w, conv_b, cln_g, cln_b, w_out):
    n = w_in.shape[0]
    assert EVEN_GROUPS == 4
    wm = jnp.stack([w_in[:, :, 0:2 * GLA_DK], w_in[:, :, 2 * GLA_DK:2 * GLA_DK + GLA_DV],
                    w_in[:, :, 2 * GLA_DK + GLA_DV:E_A0], w_in[:, :, E_CZ0:]], axis=1).astype(BF16)
    wl = jnp.pad(w_in[:, :, E_A0:E_CV0], ((0, 0), (0, 0), (0, A_LOW_PAD - GLA_RANK))).astype(BF16)
    wc = w_in[:, :, E_CV0:E_CZ0].astype(BF16)
    wa2 = jnp.pad(w_a2, ((0, 0), (0, A_LOW_PAD - GLA_RANK), (0, 0))).astype(BF16)
    cw = jnp.repeat(conv_w, SUBLANE, axis=1)
    wout = w_out.astype(BF16)
    maps = _conv_row_maps()
    vec = lambda a: a.reshape(n, 1, -1)
    stacked = [vec(norm_g), wc, wl, wm, wa2, vec(b_a), vec(gla_g), cw, vec(conv_b), vec(cln_g), vec(cln_b), wout]
    return stacked, [m[None] for m in maps]


O_GU0 = 2 * S5_WIDTH
S5_SCAN_LANES = 512
S5_PERM_T = 32


def _s5_kernel(h_ref, g_ref, w_ref, bre_ref, bim_ref, cre_ref, cim_ref, are_ref, aim_ref, d_ref, wglu_ref,
               bglu_ref, perm_ref, permt_ref, o_ref, u_ref, sz_ref, xre_ref, xim_ref, sre_ref, sim_ref, *, bsz, tt):
    @pl.when(pl.program_id(0) == 0)
    def _():
        sre_ref[...] = jnp.zeros_like(sre_ref)
        sim_ref[...] = jnp.zeros_like(sim_ref)

    xn = _rms_rows(h_ref[...].reshape(bsz * tt, D_MODEL), g_ref[...]).astype(BF16)
    u_ref[...] = jnp.dot(xn, w_ref[:, 0:S5_WIDTH], preferred_element_type=F32).astype(BF16).reshape(
        bsz, tt, S5_WIDTH)
    sz_ref[...] = _silu(jnp.dot(xn, w_ref[:, S5_WIDTH:2 * S5_WIDTH], preferred_element_type=F32)).astype(
        BF16).reshape(bsz, tt, S5_WIDTH)

    pt = S5_PERM_T
    prow = bsz * pt
    u = jnp.concatenate(
        [jnp.dot(perm_ref[...], u_ref[:, s * pt:(s + 1) * pt, :].reshape(prow, S5_WIDTH),
                 preferred_element_type=F32).astype(BF16) for s in range(tt // pt)], axis=0)
    for hf in range(2):
        ul = u[:, hf * S5_HALF_IN:(hf + 1) * S5_HALF_IN]
        sl = slice(hf * S5_HALF_ST, (hf + 1) * S5_HALF_ST)
        xre_ref[:, sl] = jnp.dot(ul, bre_ref[hf], preferred_element_type=F32)
        xim_ref[:, sl] = jnp.dot(ul, bim_ref[hf], preferred_element_type=F32)

    for cb in range(S5_LANES // S5_SCAN_LANES):
        ll = slice(cb * S5_SCAN_LANES, (cb + 1) * S5_SCAN_LANES)
        a_re = are_ref[:, ll]
        a_im = aim_ref[:, ll]
        x_re = sre_ref[:, ll]
        x_im = sim_ref[:, ll]
        for t in range(tt):
            idx = slice(t * bsz, (t + 1) * bsz)
            x_re, x_im = (a_re * x_re - a_im * x_im + xre_ref[idx, ll],
                          a_re * x_im + a_im * x_re + xim_ref[idx, ll])
            xre_ref[idx, ll] = x_re
            xim_ref[idx, ll] = x_im
        sre_ref[:, ll] = x_re
        sim_ref[:, ll] = x_im

    ys = []
    for hf in range(2):
        sl = slice(hf * S5_HALF_ST, (hf + 1) * S5_HALF_ST)
        y = jnp.dot(xre_ref[:, sl].astype(BF16), cre_ref[hf], preferred_element_type=F32)
        y = y - jnp.dot(xim_ref[:, sl].astype(BF16), cim_ref[hf], preferred_element_type=F32)
        ys.append(y)
    y = jnp.concatenate(ys, axis=-1) + d_ref[...] * u.astype(F32)
    y = _gelu_tanh(y)
    y = y * jax.nn.sigmoid(jnp.dot(y.astype(BF16), wglu_ref[...], preferred_element_type=F32) + bglu_ref[...])
    y = y.astype(BF16)
    for s in range(tt // pt):
        y_bt = jnp.dot(permt_ref[...], y[s * prow:(s + 1) * prow, :], preferred_element_type=F32)
        gate = sz_ref[:, s * pt:(s + 1) * pt, :].astype(F32)
        o_ref[:, s * pt:(s + 1) * pt, :] = (y_bt.reshape(bsz, pt, S5_WIDTH) * gate).astype(BF16)


def _s5(h3, consts, layer, *, bsz, seq, tt):
    blk = lambda c: pl.BlockSpec((bsz, tt, c), lambda i: (0, i, 0))
    return pl.pallas_call(
        functools.partial(_s5_kernel, bsz=bsz, tt=tt),
        grid=(seq // tt,),
        in_specs=[blk(D_MODEL)] + _layer_specs(consts, layer),
        out_specs=blk(S5_WIDTH),
        out_shape=jax.ShapeDtypeStruct((bsz, seq, S5_WIDTH), BF16),
        scratch_shapes=[pltpu.VMEM((bsz, tt, S5_WIDTH), BF16), pltpu.VMEM((bsz, tt, S5_WIDTH), BF16),
                        pltpu.VMEM((bsz * tt, S5_LANES), F32), pltpu.VMEM((bsz * tt, S5_LANES), F32),
                        pltpu.VMEM((bsz, S5_LANES), F32), pltpu.VMEM((bsz, S5_LANES), F32)],
        compiler_params=pltpu.CompilerParams(dimension_semantics=("arbitrary",),
                                             vmem_limit_bytes=VMEM_LIMIT),
        name="s5",
    )(h3, *consts[0], *consts[1])


def _s5_params(lam_re, lam_im, log_dt, b_re, b_im, c_re, c_im, d_skip, w_glu, b_glu, bsz):
    n = lam_re.shape[0]
    dt = jnp.exp(log_dt.astype(F32))[..., None]
    mag = jnp.exp(lam_re * dt)
    abar_re = mag * jnp.cos(lam_im * dt)
    abar_im = mag * jnp.sin(lam_im * dt)
    den = lam_re * lam_re + lam_im * lam_im
    nr, ni = abar_re - 1.0, abar_im
    coef_re = (nr * lam_re + ni * lam_im) / den
    coef_im = (ni * lam_re - nr * lam_im) / den
    bbar_re = coef_re[..., None] * b_re - coef_im[..., None] * b_im
    bbar_im = coef_re[..., None] * b_im + coef_im[..., None] * b_re
    eye = jnp.eye(S5_GROUPS, dtype=F32)

    def in_map(bb):
        full = jnp.einsum('ngph,gk->nghkp', bb, eye).reshape(n, S5_WIDTH, S5_LANES)
        return jnp.stack([full[:, :S5_HALF_IN, :S5_HALF_ST], full[:, S5_HALF_IN:, S5_HALF_ST:]], axis=1).astype(BF16)

    def out_map(cc):
        full = jnp.einsum('nghp,gk->ngpkh', cc, eye).reshape(n, S5_LANES, S5_WIDTH)
        return jnp.stack([full[:, :S5_HALF_ST, :S5_HALF_IN], full[:, S5_HALF_ST:, S5_HALF_IN:]], axis=1).astype(BF16)

    bcast = lambda a: jnp.broadcast_to(a.reshape(n, 1, S5_LANES), (n, bsz, S5_LANES))
    r = jnp.arange(bsz * S5_PERM_T)
    perm = (r[None, :] == ((r % bsz) * S5_PERM_T + r // bsz)[:, None]).astype(BF16)
    stacked = [in_map(bbar_re), in_map(bbar_im), out_map(c_re), out_map(c_im), bcast(abar_re), bcast(abar_im),
               d_skip.reshape(n, 1, -1), w_glu.astype(BF16), b_glu.reshape(n, 1, -1)]
    return stacked, [perm[None], perm.T[None]]


def _odd_mix_kernel(ys_ref, h_ref, g_ref, w_ref, lng_ref, lnb_ref, ws_ref, bs_ref, wout_ref, fg_ref,
                    o_ref, y_ref, ug_ref, vn_ref, *, tt, final_norm):
    xn = _rms_rows(h_ref[...], g_ref[...]).astype(BF16)

    def proj(j):
        return jnp.dot(xn, w_ref[:, j * SG_WIDTH:(j + 1) * SG_WIDTH], preferred_element_type=F32)

    ug_ref[...] = (proj(0) * _silu(proj(2))).astype(BF16)
    vn_ref[...] = _layernorm_rows(proj(1), lng_ref[...], lnb_ref[...]).astype(BF16)

    y_ref[:, 0:S5_WIDTH] = ys_ref[...]
    for ci in range(tt // SG_CHUNK):
        rl = slice(ci * SG_CHUNK, (ci + 1) * SG_CHUNK)
        for hd in range(SG_HEADS):
            ll = slice(hd * SG_HD, (hd + 1) * SG_HD)
            sv = jnp.dot(ws_ref[hd], vn_ref[rl, ll], preferred_element_type=F32) + bs_ref[:, ll]
            y_ref[rl, S5_WIDTH + hd * SG_HD:S5_WIDTH + (hd + 1) * SG_HD] = (
                ug_ref[rl, ll].astype(F32) * sv).astype(BF16)
    out = h_ref[...] + jnp.dot(y_ref[...], wout_ref[...], preferred_element_type=F32)
    if final_norm:
        out = _rms_rows(out, fg_ref[...])
    o_ref[...] = out


def _odd_mix(ys, h, consts, layer, *, tt, final_norm):
    bsz, seq, _ = h.shape
    row = lambda c: pl.BlockSpec((None, tt, c), lambda b, i: (b, i, 0))
    return pl.pallas_call(
        functools.partial(_odd_mix_kernel, tt=tt, final_norm=final_norm),
        grid=(bsz, seq // tt),
        in_specs=[row(S5_WIDTH), row(D_MODEL)] + _layer_specs(consts, layer),
        out_specs=row(D_MODEL),
        out_shape=jax.ShapeDtypeStruct(h.shape, F32),
        scratch_shapes=[pltpu.VMEM((tt, S5_WIDTH + SG_WIDTH), BF16), pltpu.VMEM((tt, SG_WIDTH), BF16),
                        pltpu.VMEM((tt, SG_WIDTH), BF16)],
        compiler_params=pltpu.CompilerParams(dimension_semantics=("arbitrary", "arbitrary"),
                                             vmem_limit_bytes=VMEM_LIMIT),
        name="odd_mix",
    )(ys, h, *consts[0], *consts[1])


def _odd_layer(h, s5_consts, mix_consts, layer, *, bsz, seq, final_norm):
    ys = _s5(h, s5_consts, layer, bsz=bsz, seq=seq, tt=S5_TT)
    return _odd_mix(ys, h, mix_consts, layer, tt=ODD_TT, final_norm=final_norm)


def kernel(x, norm_g, final_g, e_w_in, e_w_a2, e_b_a, e_gla_g, e_conv_w, e_conv_b, e_cln_g, e_cln_b, e_w_out, o_w_in, o_lam_re, o_lam_im, o_log_dt, o_b_re, o_b_im, o_c_re, o_c_im, o_d, o_w_glu, o_b_glu, o_sg_ln_g, o_sg_ln_b, o_w_s, o_b_s, o_w_out):
    bsz, seq, d = x.shape
    assert d == D_MODEL and bsz == SUBLANE and seq % max(EVEN_TT * EVEN_SUB, ODD_TT) == 0
    depth = norm_g.shape[0]
    assert depth % 2 == 0
    n_odd = depth // 2

    even_consts = _even_params(norm_g[0::2], e_w_in, e_w_a2, e_b_a, e_gla_g, e_conv_w, e_conv_b, e_cln_g, e_cln_b,
                               e_w_out)
    s5_consts = _s5_params(o_lam_re, o_lam_im, o_log_dt, o_b_re, o_b_im, o_c_re, o_c_im, o_d, o_w_glu, o_b_glu, bsz)
    vec = lambda a: a.reshape(n_odd, 1, -1)
    o_g, o_lng, o_lnb = vec(norm_g[1::2]), vec(o_sg_ln_g), vec(o_sg_ln_b)
    o_w_s5 = o_w_in[:, :, :O_GU0].astype(BF16)
    o_w_sg = o_w_in[:, :, O_GU0:].astype(BF16)
    causal = jnp.tril(jnp.ones((SG_CHUNK, SG_CHUNK), dtype=bool))
    ws = jnp.where(causal, o_w_s, 0.0).astype(BF16)
    bs = jnp.repeat(jnp.swapaxes(o_b_s, 1, 2), SG_HD, axis=2)
    o_wout = o_w_out.astype(BF16)
    s5_consts = ([o_g, o_w_s5] + s5_consts[0], s5_consts[1])
    mix_consts = ([o_g, o_w_sg, o_lng, o_lnb, ws, bs, o_wout], [final_g.reshape(1, 1, -1)])

    h = x
    for layer in range(depth):
        i = layer // 2
        if layer % 2 == 0:
            h = _even_layer(h, even_consts, i, bsz=bsz, seq=seq)
        else:
            h = _odd_layer(h, s5_consts, mix_consts, i, bsz=bsz, seq=seq, final_norm=layer == depth - 1)
    return h
```

```python
import functools
import math

import jax
import jax.numpy as jnp
from jax import lax
from jax.experimental import pallas as pl
from jax.experimental.pallas import tpu as pltpu

F32 = jnp.float32
BF16 = jnp.bfloat16

EPS = 1e-6
D_MODEL = 1024

GLA_HEADS = 4
GLA_DK = D_MODEL // 2
GLA_DV = D_MODEL
GLA_HK = GLA_DK // GLA_HEADS
GLA_HV = GLA_DV // GLA_HEADS
GLA_RANK = 16
GLA_TAU = 16.0
GLA_CHUNK = 64

CONV_WIDTH = D_MODEL
CONV_K = 31
CONV_HALO = 32

S5_WIDTH = D_MODEL // 2
S5_GROUP = 16
S5_GROUPS = S5_WIDTH // S5_GROUP
S5_STATE = 64
S5_LANES = S5_GROUPS * S5_STATE
S5_HALF_IN = S5_WIDTH // 2
S5_HALF_ST = S5_LANES // 2

SG_WIDTH = D_MODEL
SG_HEADS = 8
SG_HD = SG_WIDTH // SG_HEADS
SG_CHUNK = 128

LANE = 128
SUBLANE = 8
A_LOW_PAD = LANE
VMEM_LIMIT = 56 * 1024 * 1024
S5_TT = 128
ODD_TT = 1024


def _silu(x):
    return x * jax.nn.sigmoid(x)


def _rms_rows(x, g):
    return x * lax.rsqrt(jnp.mean(x * x, axis=-1, keepdims=True) + EPS) * g


def _layernorm_rows(x, g, b):
    mu = jnp.mean(x, axis=-1, keepdims=True)
    xc = x - mu
    var = jnp.mean(xc * xc, axis=-1, keepdims=True)
    return xc * lax.rsqrt(var + EPS) * g + b


def _log_sigmoid(x):
    return jnp.minimum(x, 0.0) - jnp.log1p(jnp.exp(-jnp.abs(x)))


def _gelu_tanh(x):
    c = math.sqrt(2.0 / math.pi)
    return 0.5 * x * (1.0 + jnp.tanh(c * (x + 0.044715 * (x * x * x))))


def _layer_spec(a, i):
    nd = a.ndim - 1
    return pl.BlockSpec((None,) + a.shape[1:], lambda *_: (i,) + (0,) * nd, pipeline_mode=pl.Buffered(1))


def _layer_specs(consts, layer):
    stacked, shared = consts
    return [_layer_spec(a, layer) for a in stacked] + [_layer_spec(a, 0) for a in shared]


CONV_ROWS = 64
CONV_SEG = 32
EVEN_TT = CONV_SEG * SUBLANE
CONV_EXT_ROWS = (CONV_HALO + CONV_SEG) * SUBLANE
EVEN_SUB = 4
EVEN_GROUPS = EVEN_TT // CONV_ROWS
E_A0 = 2 * GLA_DK + 2 * GLA_DV
E_CV0 = E_A0 + GLA_RANK
E_CZ0 = E_CV0 + 2 * CONV_WIDTH
P_QK, P_V, P_Z, P_CZ = range(EVEN_GROUPS)


def _conv_row_maps():
    te, seg = jnp.divmod(jnp.arange(CONV_EXT_ROWS), SUBLANE)
    src = seg * CONV_SEG + te - CONV_HALO
    sel_cur = (src[:, None] == jnp.arange(EVEN_TT)[None, :]).astype(BF16)
    sel_prev = ((src + CONV_HALO)[:, None] == jnp.arange(CONV_HALO)[None, :]).astype(BF16)
    ts, sg = jnp.divmod(jnp.arange(EVEN_TT), SUBLANE)
    unsel = (jnp.arange(EVEN_TT)[:, None] == (sg * CONV_SEG + ts)[None, :]).astype(BF16)
    return sel_cur, sel_prev, unsel


def _even_kernel(h_ref, *refs, tt, nsub):
    n_in = 15
    o_ref, st_ref, tail_ref = refs[n_in], refs[n_in + 1], refs[n_in + 3]

    @pl.when(pl.program_id(1) == 0)
    def _():
        st_ref[...] = jnp.zeros_like(st_ref)
        tail_ref[...] = jnp.zeros_like(tail_ref)

    def tile(sub, carry):
        rows = pl.ds(pl.multiple_of(sub * tt, tt), tt)
        _even_tile(h_ref.at[rows], *refs[:n_in], o_ref.at[rows], *refs[n_in + 1:], tt=tt)
        return carry

    lax.fori_loop(0, nsub, tile, 0)


def _even_tile(h_ref, g_ref, wc_ref, wl_ref, wm_ref, wa2_ref, ba_ref, glag_ref, cw_ref, cb_ref, lng_ref,
               lnb_ref, wout_ref, selc_ref, selp_ref, unsel_ref,
               o_ref, st_ref, y_ref, tail_ref, ext_ref, conv_ref, xb_ref, xn_ref, p_ref, la_ref, *, tt):
    xn = _rms_rows(h_ref[...], g_ref[...]).astype(BF16)
    xn_ref[...] = xn

    c_val = jnp.dot(xn, wc_ref[:, 0:CONV_WIDTH], preferred_element_type=F32)
    c_gate = jnp.dot(xn, wc_ref[:, CONV_WIDTH:2 * CONV_WIDTH], preferred_element_type=F32)
    u = (c_val * jax.nn.sigmoid(c_gate)).astype(BF16)
    ext_ref[...] = (jnp.dot(selc_ref[...], u, preferred_element_type=F32)
                    + jnp.dot(selp_ref[...], tail_ref[...], preferred_element_type=F32))
    tail_ref[...] = u[tt - CONV_HALO:tt, :]

    a_low = jnp.dot(xn, wl_ref[...], preferred_element_type=F32).astype(BF16)
    logit = jnp.dot(a_low, wa2_ref[...], preferred_element_type=F32) + ba_ref[...]
    la_ref[...] = _log_sigmoid(logit) * (1.0 / GLA_TAU)

    off0 = CONV_HALO - (CONV_K - 1)
    rsub = CONV_ROWS // SUBLANE

    def conv_block(rb, carry):
        p_ref[rb] = jnp.dot(xn_ref[...], wm_ref[rb], preferred_element_type=F32).astype(BF16)
        r0 = pl.multiple_of(rb * CONV_ROWS, CONV_ROWS)
        for lb in range(CONV_WIDTH // LANE):
            ll = slice(lb * LANE, (lb + 1) * LANE)
            acc = [jnp.zeros((SUBLANE, LANE), F32) for _ in range(rsub)]
            for e in range(CONV_K + rsub - 1):
                x = ext_ref[pl.ds(r0 + (off0 + e) * SUBLANE, SUBLANE), ll]
                for j in range(rsub):
                    kk = e - j
                    if 0 <= kk < CONV_K:
                        acc[j] = acc[j] + x * cw_ref[kk * SUBLANE:(kk + 1) * SUBLANE, ll]
            for j in range(rsub):
                conv_ref[pl.ds(r0 + j * SUBLANE, SUBLANE), ll] = acc[j]
        x = conv_ref[pl.ds(r0, CONV_ROWS), :] + cb_ref[...]
        xb_ref[pl.ds(r0, CONV_ROWS), :] = _silu(_layernorm_rows(x, lng_ref[...], lnb_ref[...])).astype(BF16)
        return carry

    lax.fori_loop(0, EVEN_GROUPS, conv_block, 0)
    y_conv = jnp.dot(unsel_ref[...], xb_ref[...], preferred_element_type=F32)
    y_ref[:, GLA_DV:GLA_DV + CONV_WIDTH] = (y_conv * _silu(p_ref[P_CZ].astype(F32))).astype(BF16)

    c = GLA_CHUNK
    nc = tt // c
    rows = lax.broadcasted_iota(jnp.int32, (tt, tt), 0)
    cols = lax.broadcasted_iota(jnp.int32, (tt, tt), 1)
    causal = (rows >= cols) & ((rows // c) == (cols // c))
    tri = causal.astype(BF16)
    scale = GLA_HK ** -0.5
    nt_dims = (((1,), (1,)), ((), ()))
    tn_dims = (((0,), (0,)), ((), ()))

    g = la_ref[...]
    g_hi = g.astype(BF16)
    r1 = g - g_hi.astype(F32)
    g_mid = r1.astype(BF16)
    g_lo = (r1 - g_mid.astype(F32)).astype(BF16)
    b = (jnp.dot(tri, g_hi, preferred_element_type=F32) + jnp.dot(tri, g_mid, preferred_element_type=F32)
         + jnp.dot(tri, g_lo, preferred_element_type=F32))
    b_last = [b[ci * c + c - 1:ci * c + c, :] for ci in range(nc)]
    b_last_rows = jnp.concatenate([jnp.broadcast_to(bl, (c, GLA_DK)) for bl in b_last], axis=0)
    q = p_ref[P_QK, :, 0:GLA_DK].astype(F32)
    k = p_ref[P_QK, :, GLA_DK:2 * GLA_DK].astype(F32)
    qf = (q * jnp.exp(b) * scale).astype(BF16)
    k_intra = (k * jnp.exp(-b)).astype(BF16)
    k_state = (k * jnp.exp(b_last_rows - b)).astype(BF16)
    decay = [jnp.exp(bl) for bl in b_last]

    for hd in range(GLA_HEADS):
        kl = slice(hd * GLA_HK, (hd + 1) * GLA_HK)
        vl = slice(hd * GLA_HV, (hd + 1) * GLA_HV)
        v = p_ref[P_V, :, vl]
        att = lax.dot_general(qf[:, kl], k_intra[:, kl], nt_dims, preferred_element_type=F32)
        att = jnp.where(causal, att, 0.0).astype(BF16)
        o_intra = jnp.dot(att, v, preferred_element_type=F32)
        s_t = st_ref[hd]
        o_inter = []
        for ci in range(nc):
            rl = slice(ci * c, (ci + 1) * c)
            o_inter.append(lax.dot_general(qf[rl, kl], s_t.astype(BF16), nt_dims, preferred_element_type=F32))
            kv_t = lax.dot_general(v[rl, :], k_state[rl, kl], tn_dims, preferred_element_type=F32)
            s_t = s_t * decay[ci][:, kl] + kv_t
        st_ref[hd] = s_t
        o = o_intra + jnp.concatenate(o_inter, axis=0)
        o = o * lax.rsqrt(jnp.mean(o * o, axis=-1, keepdims=True) + EPS) * glag_ref[:, vl]
        y_ref[:, vl] = (o * _silu(p_ref[P_Z, :, vl].astype(F32))).astype(BF16)

    o_ref[...] = h_ref[...] + jnp.dot(y_ref[...], wout_ref[...], preferred_element_type=F32)


def _even_layer(h, consts, layer, *, bsz, seq):
    tt = EVEN_TT
    row = pl.BlockSpec((None, tt * EVEN_SUB, D_MODEL), lambda b, i: (b, i, 0))
    return pl.pallas_call(
        functools.partial(_even_kernel, tt=tt, nsub=EVEN_SUB),
        grid=(bsz, seq // (tt * EVEN_SUB)),
        in_specs=[row] + _layer_specs(consts, layer),
        out_specs=row,
        out_shape=jax.ShapeDtypeStruct(h.shape, F32),
        scratch_shapes=[pltpu.VMEM((GLA_HEADS, GLA_HV, GLA_HK), F32),
                        pltpu.VMEM((tt, GLA_DV + CONV_WIDTH), BF16),
                        pltpu.VMEM((CONV_HALO, CONV_WIDTH), BF16),
                        pltpu.VMEM((CONV_EXT_ROWS, CONV_WIDTH), F32),
                        pltpu.VMEM((tt, CONV_WIDTH), F32),
                        pltpu.VMEM((tt, CONV_WIDTH), BF16),
                        pltpu.VMEM((tt, D_MODEL), BF16),
                        pltpu.VMEM((EVEN_GROUPS, tt, D_MODEL), BF16),
                        pltpu.VMEM((tt, GLA_DK), F32)],
        compiler_params=pltpu.CompilerParams(dimension_semantics=("arbitrary", "arbitrary"),
                                             vmem_limit_bytes=VMEM_LIMIT),
        name="even_layer",
    )(h, *consts[0], *consts[1])


def _even_params(norm_g, w_in, w_a2, b_a, gla_g, conv_w, conv_b, cln_g, cln_b, w_out):
    n = w_in.shape[0]
    assert EVEN_GROUPS == 4
    wm = jnp.stack([w_in[:, :, 0:2 * GLA_DK], w_in[:, :, 2 * GLA_DK:2 * GLA_DK + GLA_DV],
                    w_in[:, :, 2 * GLA_DK + GLA_DV:E_A0], w_in[:, :, E_CZ0:]], axis=1).astype(BF16)
    wl = jnp.pad(w_in[:, :, E_A0:E_CV0], ((0, 0), (0, 0), (0, A_LOW_PAD - GLA_RANK))).astype(BF16)
    wc = w_in[:, :, E_CV0:E_CZ0].astype(BF16)
    wa2 = jnp.pad(w_a2, ((0, 0), (0, A_LOW_PAD - GLA_RANK), (0, 0))).astype(BF16)
    cw = jnp.repeat(conv_w, SUBLANE, axis=1)
    wout = w_out.astype(BF16)
    maps = _conv_row_maps()
    vec = lambda a: a.reshape(n, 1, -1)
    stacked = [vec(norm_g), wc, wl, wm, wa2, vec(b_a), vec(gla_g), cw, vec(conv_b), vec(cln_g), vec(cln_b), wout]
    return stacked, [m[None] for m in maps]


O_GU0 = 2 * S5_WIDTH
S5_SCAN_LANES = 512
S5_PERM_T = 32


def _s5_kernel(h_ref, g_ref, w_ref, bre_ref, bim_ref, cre_ref, cim_ref, are_ref, aim_ref, d_ref, wglu_ref,
               bglu_ref, perm_ref, permt_ref, o_ref, u_ref, sz_ref, xre_ref, xim_ref, sre_ref, sim_ref, *, bsz, tt):
    @pl.when(pl.program_id(0) == 0)
    def _():
        sre_ref[...] = jnp.zeros_like(sre_ref)
        sim_ref[...] = jnp.zeros_like(sim_ref)

    xn = _rms_rows(h_ref[...].reshape(bsz * tt, D_MODEL), g_ref[...]).astype(BF16)
    u_ref[...] = jnp.dot(xn, w_ref[:, 0:S5_WIDTH], preferred_element_type=F32).astype(BF16).reshape(
        bsz, tt, S5_WIDTH)
    sz_ref[...] = _silu(jnp.dot(xn, w_ref[:, S5_WIDTH:2 * S5_WIDTH], preferred_element_type=F32)).astype(
        BF16).reshape(bsz, tt, S5_WIDTH)

    pt = S5_PERM_T
    prow = bsz * pt
    u = jnp.concatenate(
        [jnp.dot(perm_ref[...], u_ref[:, s * pt:(s + 1) * pt, :].reshape(prow, S5_WIDTH),
                 preferred_element_type=F32).astype(BF16) for s in range(tt // pt)], axis=0)
    for hf in range(2):
        ul = u[:, hf * S5_HALF_IN:(hf + 1) * S5_HALF_IN]
        sl = slice(hf * S5_HALF_ST, (hf + 1) * S5_HALF_ST)
        xre_ref[:, sl] = jnp.dot(ul, bre_ref[hf], preferred_element_type=F32)
        xim_ref[:, sl] = jnp.dot(ul, bim_ref[hf], preferred_element_type=F32)

    for cb in range(S5_LANES // S5_SCAN_LANES):
        ll = slice(cb * S5_SCAN_LANES, (cb + 1) * S5_SCAN_LANES)
        a_re = are_ref[:, ll]
        a_im = aim_ref[:, ll]
        x_re = sre_ref[:, ll]
        x_im = sim_ref[:, ll]
        for t in range(tt):
            idx = slice(t * bsz, (t + 1) * bsz)
            x_re, x_im = (a_re * x_re - a_im * x_im + xre_ref[idx, ll],
                          a_re * x_im + a_im * x_re + xim_ref[idx, ll])
            xre_ref[idx, ll] = x_re
            xim_ref[idx, ll] = x_im
        sre_ref[:, ll] = x_re
        sim_ref[:, ll] = x_im

    ys = []
    for hf in range(2):
        sl = slice(hf * S5_HALF_ST, (hf + 1) * S5_HALF_ST)
        y = jnp.dot(xre_ref[:, sl].astype(BF16), cre_ref[hf], preferred_element_type=F32)
        y = y - jnp.dot(xim_ref[:, sl].astype(BF16), cim_ref[hf], preferred_element_type=F32)
        ys.append(y)
    y = jnp.concatenate(ys, axis=-1) + d_ref[...] * u.astype(F32)
    y = _gelu_tanh(y)
    y = y * jax.nn.sigmoid(jnp.dot(y.astype(BF16), wglu_ref[...], preferred_element_type=F32) + bglu_ref[...])
    y = y.astype(BF16)
    for s in range(tt // pt):
        y_bt = jnp.dot(permt_ref[...], y[s * prow:(s + 1) * prow, :], preferred_element_type=F32)
        gate = sz_ref[:, s * pt:(s + 1) * pt, :].astype(F32)
        o_ref[:, s * pt:(s + 1) * pt, :] = (y_bt.reshape(bsz, pt, S5_WIDTH) * gate).astype(BF16)


def _s5(h3, consts, layer, *, bsz, seq, tt):
    blk = lambda c: pl.BlockSpec((bsz, tt, c), lambda i: (0, i, 0))
    return pl.pallas_call(
        functools.partial(_s5_kernel, bsz=bsz, tt=tt),
        grid=(seq // tt,),
        in_specs=[blk(D_MODEL)] + _layer_specs(consts, layer),
        out_specs=blk(S5_WIDTH),
        out_shape=jax.ShapeDtypeStruct((bsz, seq, S5_WIDTH), BF16),
        scratch_shapes=[pltpu.VMEM((bsz, tt, S5_WIDTH), BF16), pltpu.VMEM((bsz, tt, S5_WIDTH), BF16),
                        pltpu.VMEM((bsz * tt, S5_LANES), F32), pltpu.VMEM((bsz * tt, S5_LANES), F32),
                        pltpu.VMEM((bsz, S5_LANES), F32), pltpu.VMEM((bsz, S5_LANES), F32)],
        compiler_params=pltpu.CompilerParams(dimension_semantics=("arbitrary",),
                                             vmem_limit_bytes=VMEM_LIMIT),
        name="s5",
    )(h3, *consts[0], *consts[1])


def _s5_params(lam_re, lam_im, log_dt, b_re, b_im, c_re, c_im, d_skip, w_glu, b_glu, bsz):
    n = lam_re.shape[0]
    dt = jnp.exp(log_dt.astype(F32))[..., None]
    mag = jnp.exp(lam_re * dt)
    abar_re = mag * jnp.cos(lam_im * dt)
    abar_im = mag * jnp.sin(lam_im * dt)
    den = lam_re * lam_re + lam_im * lam_im
    nr, ni = abar_re - 1.0, abar_im
    coef_re = (nr * lam_re + ni * lam_im) / den
    coef_im = (ni * lam_re - nr * lam_im) / den
    bbar_re = coef_re[..., None] * b_re - coef_im[..., None] * b_im
    bbar_im = coef_re[..., None] * b_im + coef_im[..., None] * b_re
    hg = S5_GROUPS // 2

    def block_diag(rows, row_group, lane_group):
        tiled = jnp.tile(rows, (1, 1, 1, hg))
        keep = ((jnp.arange(hg * row_group) // row_group)[:, None]
                == (jnp.arange(hg * lane_group) // lane_group)[None, :])
        return jnp.where(keep, tiled, 0.0).astype(BF16)

    def in_map(bb):
        return block_diag(jnp.swapaxes(bb, 2, 3).reshape(n, 2, hg * S5_GROUP, S5_STATE), S5_GROUP, S5_STATE)

    def out_map(cc):
        return block_diag(jnp.swapaxes(cc, 2, 3).reshape(n, 2, hg * S5_STATE, S5_GROUP), S5_STATE, S5_GROUP)

    bcast = lambda a: jnp.broadcast_to(a.reshape(n, 1, S5_LANES), (n, bsz, S5_LANES))
    r = jnp.arange(bsz * S5_PERM_T)
    perm = (r[None, :] == ((r % bsz) * S5_PERM_T + r // bsz)[:, None]).astype(BF16)
    stacked = [in_map(bbar_re), in_map(bbar_im), out_map(c_re), out_map(c_im), bcast(abar_re), bcast(abar_im),
               d_skip.reshape(n, 1, -1), w_glu.astype(BF16), b_glu.reshape(n, 1, -1)]
    return stacked, [perm[None], perm.T[None]]


def _odd_mix_kernel(ys_ref, h_ref, g_ref, w_ref, lng_ref, lnb_ref, ws_ref, bs_ref, wout_ref, fg_ref,
                    o_ref, y_ref, ug_ref, vn_ref, *, tt, final_norm):
    xn = _rms_rows(h_ref[...], g_ref[...]).astype(BF16)

    def proj(j):
        return jnp.dot(xn, w_ref[:, j * SG_WIDTH:(j + 1) * SG_WIDTH], preferred_element_type=F32)

    ug_ref[...] = (proj(0) * _silu(proj(2))).astype(BF16)
    vn_ref[...] = _layernorm_rows(proj(1), lng_ref[...], lnb_ref[...]).astype(BF16)

    y_ref[:, 0:S5_WIDTH] = ys_ref[...]
    for ci in range(tt // SG_CHUNK):
        rl = slice(ci * SG_CHUNK, (ci + 1) * SG_CHUNK)
        for hd in range(SG_HEADS):
            ll = slice(hd * SG_HD, (hd + 1) * SG_HD)
            sv = jnp.dot(ws_ref[hd], vn_ref[rl, ll], preferred_element_type=F32) + bs_ref[:, ll]
            y_ref[rl, S5_WIDTH + hd * SG_HD:S5_WIDTH + (hd + 1) * SG_HD] = (
                ug_ref[rl, ll].astype(F32) * sv).astype(BF16)
    out = h_ref[...] + jnp.dot(y_ref[...], wout_ref[...], preferred_element_type=F32)
    if final_norm:
        out = _rms_rows(out, fg_ref[...])
    o_ref[...] = out


def _odd_mix(ys, h, consts, layer, *, tt, final_norm):
    bsz, seq, _ = h.shape
    row = lambda c: pl.BlockSpec((None, tt, c), lambda b, i: (b, i, 0))
    return pl.pallas_call(
        functools.partial(_odd_mix_kernel, tt=tt, final_norm=final_norm),
        grid=(bsz, seq // tt),
        in_specs=[row(S5_WIDTH), row(D_MODEL)] + _layer_specs(consts, layer),
        out_specs=row(D_MODEL),
        out_shape=jax.ShapeDtypeStruct(h.shape, F32),
        scratch_shapes=[pltpu.VMEM((tt, S5_WIDTH + SG_WIDTH), BF16), pltpu.VMEM((tt, SG_WIDTH), BF16),
                        pltpu.VMEM((tt, SG_WIDTH), BF16)],
        compiler_params=pltpu.CompilerParams(dimension_semantics=("arbitrary", "arbitrary"),
                                             vmem_limit_bytes=VMEM_LIMIT),
        name="odd_mix",
    )(ys, h, *consts[0], *consts[1])


def _odd_layer(h, s5_consts, mix_consts, layer, *, bsz, seq, final_norm):
    ys = _s5(h, s5_consts, layer, bsz=bsz, seq=seq, tt=S5_TT)
    return _odd_mix(ys, h, mix_consts, layer, tt=ODD_TT, final_norm=final_norm)


def kernel(x, norm_g, final_g, e_w_in, e_w_a2, e_b_a, e_gla_g, e_conv_w, e_conv_b, e_cln_g, e_cln_b, e_w_out, o_w_in, o_lam_re, o_lam_im, o_log_dt, o_b_re, o_b_im, o_c_re, o_c_im, o_d, o_w_glu, o_b_glu, o_sg_ln_g, o_sg_ln_b, o_w_s, o_b_s, o_w_out):
    bsz, seq, d = x.shape
    assert d == D_MODEL and bsz == SUBLANE and seq % max(EVEN_TT * EVEN_SUB, ODD_TT) == 0
    depth = norm_g.shape[0]
    assert depth % 2 == 0
    n_odd = depth // 2

    even_consts = _even_params(norm_g[0::2], e_w_in, e_w_a2, e_b_a, e_gla_g, e_conv_w, e_conv_b, e_cln_g, e_cln_b,
                               e_w_out)
    s5_consts = _s5_params(o_lam_re, o_lam_im, o_log_dt, o_b_re, o_b_im, o_c_re, o_c_im, o_d, o_w_glu, o_b_glu, bsz)
    vec = lambda a: a.reshape(n_odd, 1, -1)
    o_g, o_lng, o_lnb = vec(norm_g[1::2]), vec(o_sg_ln_g), vec(o_sg_ln_b)
    o_w_s5 = o_w_in[:, :, :O_GU0].astype(BF16)
    o_w_sg = o_w_in[:, :, O_GU0:].astype(BF16)
    causal = jnp.tril(jnp.ones((SG_CHUNK, SG_CHUNK), dtype=bool))
    ws = jnp.where(causal, o_w_s, 0.0).astype(BF16)
    bs = jnp.repeat(jnp.swapaxes(o_b_s, 1, 2), SG_HD, axis=2)
    o_wout = o_w_out.astype(BF16)
    s5_consts = ([o_g, o_w_s5] + s5_consts[0], s5_consts[1])
    mix_consts = ([o_g, o_w_sg, o_lng, o_lnb, ws, bs, o_wout], [final_g.reshape(1, 1, -1)])

    h = x
    for layer in range(depth):
        i = layer // 2
        if layer % 2 == 0:
            h = _even_layer(h, even_consts, i, bsz=bsz, seq=seq)
        else:
            h = _odd_layer(h, s5_consts, mix_consts, i, bsz=bsz, seq=seq, final_norm=layer == depth - 1)
    return h
```

```python
import functools
import math

import jax
import jax.numpy as jnp
from jax import lax
from jax.experimental import pallas as pl
from jax.experimental.pallas import tpu as pltpu

F32 = jnp.float32
BF16 = jnp.bfloat16

EPS = 1e-6
D_MODEL = 1024

GLA_HEADS = 4
GLA_DK = D_MODEL // 2
GLA_DV = D_MODEL
GLA_HK = GLA_DK // GLA_HEADS
GLA_HV = GLA_DV // GLA_HEADS
GLA_RANK = 16
GLA_TAU = 16.0
GLA_CHUNK = 64

CONV_WIDTH = D_MODEL
CONV_K = 31
CONV_HALO = 32

S5_WIDTH = D_MODEL // 2
S5_GROUP = 16
S5_GROUPS = S5_WIDTH // S5_GROUP
S5_STATE = 64
S5_LANES = S5_GROUPS * S5_STATE
S5_HALF_IN = S5_WIDTH // 2
S5_HALF_ST = S5_LANES // 2

SG_WIDTH = D_MODEL
SG_HEADS = 8
SG_HD = SG_WIDTH // SG_HEADS
SG_CHUNK = 128

LANE = 128
SUBLANE = 8
A_LOW_PAD = LANE
VMEM_LIMIT = 56 * 1024 * 1024
S5_TT = 128
ODD_TT = 1024


def _silu(x):
    return x * jax.nn.sigmoid(x)


def _rms_rows(x, g):
    return x * lax.rsqrt(jnp.mean(x * x, axis=-1, keepdims=True) + EPS) * g


def _layernorm_rows(x, g, b):
    mu = jnp.mean(x, axis=-1, keepdims=True)
    xc = x - mu
    var = jnp.mean(xc * xc, axis=-1, keepdims=True)
    return xc * lax.rsqrt(var + EPS) * g + b


def _log_sigmoid(x):
    return jnp.minimum(x, 0.0) - jnp.log1p(jnp.exp(-jnp.abs(x)))


def _gelu_tanh(x):
    c = math.sqrt(2.0 / math.pi)
    return 0.5 * x * (1.0 + jnp.tanh(c * (x + 0.044715 * (x * x * x))))


def _layer_spec(a, i):
    nd = a.ndim - 1
    return pl.BlockSpec((None,) + a.shape[1:], lambda *_: (i,) + (0,) * nd, pipeline_mode=pl.Buffered(1))


def _layer_specs(consts, layer):
    stacked, shared = consts
    return [_layer_spec(a, layer) for a in stacked] + [_layer_spec(a, 0) for a in shared]


CONV_ROWS = 64
CONV_SEG = 32
EVEN_TT = CONV_SEG * SUBLANE
CONV_EXT_ROWS = (CONV_HALO + CONV_SEG) * SUBLANE
EVEN_SUB = 4
EVEN_GROUPS = EVEN_TT // CONV_ROWS
E_A0 = 2 * GLA_DK + 2 * GLA_DV
E_CV0 = E_A0 + GLA_RANK
E_CZ0 = E_CV0 + 2 * CONV_WIDTH
P_QK, P_V, P_Z, P_CZ = range(EVEN_GROUPS)


def _conv_row_maps():
    te, seg = jnp.divmod(jnp.arange(CONV_EXT_ROWS), SUBLANE)
    src = seg * CONV_SEG + te - CONV_HALO
    sel_cur = (src[:, None] == jnp.arange(EVEN_TT)[None, :]).astype(BF16)
    sel_prev = ((src + CONV_HALO)[:, None] == jnp.arange(CONV_HALO)[None, :]).astype(BF16)
    ts, sg = jnp.divmod(jnp.arange(EVEN_TT), SUBLANE)
    unsel = (jnp.arange(EVEN_TT)[:, None] == (sg * CONV_SEG + ts)[None, :]).astype(BF16)
    return sel_cur, sel_prev, unsel


def _even_kernel(h_ref, *refs, tt, nsub):
    n_in = 15
    o_ref, st_ref, tail_ref = refs[n_in], refs[n_in + 1], refs[n_in + 3]

    @pl.when(pl.program_id(1) == 0)
    def _():
        st_ref[...] = jnp.zeros_like(st_ref)
        tail_ref[...] = jnp.zeros_like(tail_ref)

    def tile(sub, carry):
        rows = pl.ds(pl.multiple_of(sub * tt, tt), tt)
        _even_tile(h_ref.at[rows], *refs[:n_in], o_ref.at[rows], *refs[n_in + 1:], tt=tt)
        return carry

    lax.fori_loop(0, nsub, tile, 0)


def _even_tile(h_ref, g_ref, wc_ref, wl_ref, wm_ref, wa2_ref, ba_ref, glag_ref, cw_ref, cb_ref, lng_ref,
               lnb_ref, wout_ref, selc_ref, selp_ref, unsel_ref,
               o_ref, st_ref, y_ref, tail_ref, ext_ref, conv_ref, xb_ref, xn_ref, p_ref, la_ref, *, tt):
    xn = _rms_rows(h_ref[...], g_ref[...]).astype(BF16)
    xn_ref[...] = xn

    c_val = jnp.dot(xn, wc_ref[:, 0:CONV_WIDTH], preferred_element_type=F32)
    c_gate = jnp.dot(xn, wc_ref[:, CONV_WIDTH:2 * CONV_WIDTH], preferred_element_type=F32)
    u = (c_val * jax.nn.sigmoid(c_gate)).astype(BF16)
    ext_ref[...] = (jnp.dot(selc_ref[...], u, preferred_element_type=F32)
                    + jnp.dot(selp_ref[...], tail_ref[...], preferred_element_type=F32))
    tail_ref[...] = u[tt - CONV_HALO:tt, :]

    a_low = jnp.dot(xn, wl_ref[...], preferred_element_type=F32).astype(BF16)
    logit = jnp.dot(a_low, wa2_ref[...], preferred_element_type=F32) + ba_ref[...]
    la_ref[...] = _log_sigmoid(logit) * (1.0 / GLA_TAU)

    off0 = CONV_HALO - (CONV_K - 1)
    rsub = CONV_ROWS // SUBLANE

    n_kc = 4
    kc = D_MODEL // n_kc

    def conv_block(rb, carry):
        r0 = pl.multiple_of(rb * CONV_ROWS, CONV_ROWS)
        proj = None
        acc = None
        for lb in range(CONV_WIDTH // LANE):
            if lb < n_kc:
                ci = lb
                cl = slice(ci * kc, (ci + 1) * kc)
                if acc is not None:
                    bits = lax.bitcast_convert_type(jnp.concatenate([acc[0], acc[1]], axis=0), jnp.uint32)
                    zero = lax.bitcast_convert_type((bits >> 16) >> 16, F32).astype(BF16)
                    head = (slice(0, 2 * SUBLANE), slice(ci * kc, ci * kc + LANE))
                    xn_ref[head] = xn_ref[head] + zero
                part = jnp.dot(xn_ref[:, cl], wm_ref[rb, cl, :], preferred_element_type=F32)
                proj = part if proj is None else proj + part
            ll = slice(lb * LANE, (lb + 1) * LANE)
            acc = [jnp.zeros((SUBLANE, LANE), F32) for _ in range(rsub)]
            for e in range(CONV_K + rsub - 1):
                x = ext_ref[pl.ds(r0 + (off0 + e) * SUBLANE, SUBLANE), ll]
                for j in range(rsub):
                    kk = e - j
                    if 0 <= kk < CONV_K:
                        acc[j] = acc[j] + x * cw_ref[kk * SUBLANE:(kk + 1) * SUBLANE, ll]
            for j in range(rsub):
                conv_ref[pl.ds(r0 + j * SUBLANE, SUBLANE), ll] = acc[j]
        p_ref[rb] = proj.astype(BF16)
        x = conv_ref[pl.ds(r0, CONV_ROWS), :] + cb_ref[...]
        xb_ref[pl.ds(r0, CONV_ROWS), :] = _silu(_layernorm_rows(x, lng_ref[...], lnb_ref[...])).astype(BF16)
        return carry

    lax.fori_loop(0, EVEN_GROUPS, conv_block, 0)
    y_conv = jnp.dot(unsel_ref[...], xb_ref[...], preferred_element_type=F32)
    y_ref[:, GLA_DV:GLA_DV + CONV_WIDTH] = (y_conv * _silu(p_ref[P_CZ].astype(F32))).astype(BF16)

    c = GLA_CHUNK
    nc = tt // c
    rows = lax.broadcasted_iota(jnp.int32, (tt, tt), 0)
    cols = lax.broadcasted_iota(jnp.int32, (tt, tt), 1)
    causal = (rows >= cols) & ((rows // c) == (cols // c))
    tri = causal.astype(BF16)
    scale = GLA_HK ** -0.5
    nt_dims = (((1,), (1,)), ((), ()))
    tn_dims = (((0,), (0,)), ((), ()))

    g = la_ref[...]
    g_hi = g.astype(BF16)
    r1 = g - g_hi.astype(F32)
    g_mid = r1.astype(BF16)
    g_lo = (r1 - g_mid.astype(F32)).astype(BF16)
    b = (jnp.dot(tri, g_hi, preferred_element_type=F32) + jnp.dot(tri, g_mid, preferred_element_type=F32)
         + jnp.dot(tri, g_lo, preferred_element_type=F32))
    b_last = [b[ci * c + c - 1:ci * c + c, :] for ci in range(nc)]
    b_last_rows = jnp.concatenate([jnp.broadcast_to(bl, (c, GLA_DK)) for bl in b_last], axis=0)
    q = p_ref[P_QK, :, 0:GLA_DK].astype(F32)
    k = p_ref[P_QK, :, GLA_DK:2 * GLA_DK].astype(F32)
    qf = (q * jnp.exp(b) * scale).astype(BF16)
    k_intra = (k * jnp.exp(-b)).astype(BF16)
    k_state = (k * jnp.exp(b_last_rows - b)).astype(BF16)
    decay = [jnp.exp(bl) for bl in b_last]

    for hd in range(GLA_HEADS):
        kl = slice(hd * GLA_HK, (hd + 1) * GLA_HK)
        vl = slice(hd * GLA_HV, (hd + 1) * GLA_HV)
        v = p_ref[P_V, :, vl]
        att = lax.dot_general(qf[:, kl], k_intra[:, kl], nt_dims, preferred_element_type=F32)
        att = jnp.where(causal, att, 0.0).astype(BF16)
        o_intra = jnp.dot(att, v, preferred_element_type=F32)
        s_t = st_ref[hd]
        o_inter = []
        for ci in range(nc):
            rl = slice(ci * c, (ci + 1) * c)
            o_inter.append(lax.dot_general(qf[rl, kl], s_t.astype(BF16), nt_dims, preferred_element_type=F32))
            kv_t = lax.dot_general(v[rl, :], k_state[rl, kl], tn_dims, preferred_element_type=F32)
            s_t = s_t * decay[ci][:, kl] + kv_t
        st_ref[hd] = s_t
        o = o_intra + jnp.concatenate(o_inter, axis=0)
        o = o * lax.rsqrt(jnp.mean(o * o, axis=-1, keepdims=True) + EPS) * glag_ref[:, vl]
        y_ref[:, vl] = (o * _silu(p_ref[P_Z, :, vl].astype(F32))).astype(BF16)

    o_ref[...] = h_ref[...] + jnp.dot(y_ref[...], wout_ref[...], preferred_element_type=F32)


def _even_layer(h, consts, layer, *, bsz, seq):
    tt = EVEN_TT
    row = pl.BlockSpec((None, tt * EVEN_SUB, D_MODEL), lambda b, i: (b, i, 0))
    return pl.pallas_call(
        functools.partial(_even_kernel, tt=tt, nsub=EVEN_SUB),
        grid=(bsz, seq // (tt * EVEN_SUB)),
        in_specs=[row] + _layer_specs(consts, layer),
        out_specs=row,
        out_shape=jax.ShapeDtypeStruct(h.shape, F32),
        scratch_shapes=[pltpu.VMEM((GLA_HEADS, GLA_HV, GLA_HK), F32),
                        pltpu.VMEM((tt, GLA_DV + CONV_WIDTH), BF16),
                        pltpu.VMEM((CONV_HALO, CONV_WIDTH), BF16),
                        pltpu.VMEM((CONV_EXT_ROWS, CONV_WIDTH), F32),
                        pltpu.VMEM((tt, CONV_WIDTH), F32),
                        pltpu.VMEM((tt, CONV_WIDTH), BF16),
                        pltpu.VMEM((tt, D_MODEL), BF16),
                        pltpu.VMEM((EVEN_GROUPS, tt, D_MODEL), BF16),
                        pltpu.VMEM((tt, GLA_DK), F32)],
        compiler_params=pltpu.CompilerParams(dimension_semantics=("arbitrary", "arbitrary"),
                                             vmem_limit_bytes=VMEM_LIMIT),
        name="even_layer",
    )(h, *consts[0], *consts[1])


def _even_params(norm_g, w_in, w_a2, b_a, gla_g, conv_w, conv_b, cln_g, cln_b, w_out):
    n = w_in.shape[0]
    assert EVEN_GROUPS == 4
    wm = jnp.stack([w_in[:, :, 0:2 * GLA_DK], w_in[:, :, 2 * GLA_DK:2 * GLA_DK + GLA_DV],
                    w_in[:, :, 2 * GLA_DK + GLA_DV:E_A0], w_in[:, :, E_CZ0:]], axis=1).astype(BF16)
    wl = jnp.pad(w_in[:, :, E_A0:E_CV0], ((0, 0), (0, 0), (0, A_LOW_PAD - GLA_RANK))).astype(BF16)
    wc = w_in[:, :, E_CV0:E_CZ0].astype(BF16)
    wa2 = jnp.pad(w_a2, ((0, 0), (0, A_LOW_PAD - GLA_RANK), (0, 0))).astype(BF16)
    cw = jnp.repeat(conv_w, SUBLANE, axis=1)
    wout = w_out.astype(BF16)
    maps = _conv_row_maps()
    vec = lambda a: a.reshape(n, 1, -1)
    stacked = [vec(norm_g), wc, wl, wm, wa2, vec(b_a), vec(gla_g), cw, vec(conv_b), vec(cln_g), vec(cln_b), wout]
    return stacked, [m[None] for m in maps]


O_GU0 = 2 * S5_WIDTH
S5_SCAN_LANES = 512
S5_PERM_T = 32


def _s5_kernel(h_ref, g_ref, w_ref, bre_ref, bim_ref, cre_ref, cim_ref, are_ref, aim_ref, d_ref, wglu_ref,
               bglu_ref, perm_ref, permt_ref, o_ref, u_ref, sz_ref, xre_ref, xim_ref, sre_ref, sim_ref, *, bsz, tt):
    @pl.when(pl.program_id(0) == 0)
    def _():
        sre_ref[...] = jnp.zeros_like(sre_ref)
        sim_ref[...] = jnp.zeros_like(sim_ref)

    xn = _rms_rows(h_ref[...].reshape(bsz * tt, D_MODEL), g_ref[...]).astype(BF16)
    u_ref[...] = jnp.dot(xn, w_ref[:, 0:S5_WIDTH], preferred_element_type=F32).astype(BF16).reshape(
        bsz, tt, S5_WIDTH)
    sz_ref[...] = _silu(jnp.dot(xn, w_ref[:, S5_WIDTH:2 * S5_WIDTH], preferred_element_type=F32)).astype(
        BF16).reshape(bsz, tt, S5_WIDTH)

    pt = S5_PERM_T
    prow = bsz * pt
    u = jnp.concatenate(
        [jnp.dot(perm_ref[...], u_ref[:, s * pt:(s + 1) * pt, :].reshape(prow, S5_WIDTH),
                 preferred_element_type=F32).astype(BF16) for s in range(tt // pt)], axis=0)
    for hf in range(2):
        ul = u[:, hf * S5_HALF_IN:(hf + 1) * S5_HALF_IN]
        sl = slice(hf * S5_HALF_ST, (hf + 1) * S5_HALF_ST)
        xre_ref[:, sl] = jnp.dot(ul, bre_ref[hf], preferred_element_type=F32)
        xim_ref[:, sl] = jnp.dot(ul, bim_ref[hf], preferred_element_type=F32)

    for cb in range(S5_LANES // S5_SCAN_LANES):
        ll = slice(cb * S5_SCAN_LANES, (cb + 1) * S5_SCAN_LANES)
        a_re = are_ref[:, ll]
        a_im = aim_ref[:, ll]
        x_re = sre_ref[:, ll]
        x_im = sim_ref[:, ll]
        for t in range(tt):
            idx = slice(t * bsz, (t + 1) * bsz)
            x_re, x_im = (a_re * x_re - a_im * x_im + xre_ref[idx, ll],
                          a_re * x_im + a_im * x_re + xim_ref[idx, ll])
            xre_ref[idx, ll] = x_re
            xim_ref[idx, ll] = x_im
        sre_ref[:, ll] = x_re
        sim_ref[:, ll] = x_im

    ys = []
    for hf in range(2):
        sl = slice(hf * S5_HALF_ST, (hf + 1) * S5_HALF_ST)
        y = jnp.dot(xre_ref[:, sl].astype(BF16), cre_ref[hf], preferred_element_type=F32)
        y = y - jnp.dot(xim_ref[:, sl].astype(BF16), cim_ref[hf], preferred_element_type=F32)
        ys.append(y)
    y = jnp.concatenate(ys, axis=-1) + d_ref[...] * u.astype(F32)
    y = _gelu_tanh(y)
    y = y * jax.nn.sigmoid(jnp.dot(y.astype(BF16), wglu_ref[...], preferred_element_type=F32) + bglu_ref[...])
    y = y.astype(BF16)
    for s in range(tt // pt):
        y_bt = jnp.dot(permt_ref[...], y[s * prow:(s + 1) * prow, :], preferred_element_type=F32)
        gate = sz_ref[:, s * pt:(s + 1) * pt, :].astype(F32)
        o_ref[:, s * pt:(s + 1) * pt, :] = (y_bt.reshape(bsz, pt, S5_WIDTH) * gate).astype(BF16)


def _s5(h3, consts, layer, *, bsz, seq, tt):
    blk = lambda c: pl.BlockSpec((bsz, tt, c), lambda i: (0, i, 0))
    return pl.pallas_call(
        functools.partial(_s5_kernel, bsz=bsz, tt=tt),
        grid=(seq // tt,),
        in_specs=[blk(D_MODEL)] + _layer_specs(consts, layer),
        out_specs=blk(S5_WIDTH),
        out_shape=jax.ShapeDtypeStruct((bsz, seq, S5_WIDTH), BF16),
        scratch_shapes=[pltpu.VMEM((bsz, tt, S5_WIDTH), BF16), pltpu.VMEM((bsz, tt, S5_WIDTH), BF16),
                        pltpu.VMEM((bsz * tt, S5_LANES), F32), pltpu.VMEM((bsz * tt, S5_LANES), F32),
                        pltpu.VMEM((bsz, S5_LANES), F32), pltpu.VMEM((bsz, S5_LANES), F32)],
        compiler_params=pltpu.CompilerParams(dimension_semantics=("arbitrary",),
                                             vmem_limit_bytes=VMEM_LIMIT),
        name="s5",
    )(h3, *consts[0], *consts[1])


def _s5_params(lam_re, lam_im, log_dt, b_re, b_im, c_re, c_im, d_skip, w_glu, b_glu, bsz):
    n = lam_re.shape[0]
    dt = jnp.exp(log_dt.astype(F32))[..., None]
    mag = jnp.exp(lam_re * dt)
    abar_re = mag * jnp.cos(lam_im * dt)
    abar_im = mag * jnp.sin(lam_im * dt)
    den = lam_re * lam_re + lam_im * lam_im
    nr, ni = abar_re - 1.0, abar_im
    coef_re = (nr * lam_re + ni * lam_im) / den
    coef_im = (ni * lam_re - nr * lam_im) / den
    bbar_re = coef_re[..., None] * b_re - coef_im[..., None] * b_im
    bbar_im = coef_re[..., None] * b_im + coef_im[..., None] * b_re
    hg = S5_GROUPS // 2

    def block_diag(rows, row_group, lane_group):
        tiled = jnp.tile(rows, (1, 1, 1, hg))
        keep = ((jnp.arange(hg * row_group) // row_group)[:, None]
                == (jnp.arange(hg * lane_group) // lane_group)[None, :])
        return jnp.where(keep, tiled, 0.0).astype(BF16)

    def in_map(bb):
        return block_diag(jnp.swapaxes(bb, 2, 3).reshape(n, 2, hg * S5_GROUP, S5_STATE), S5_GROUP, S5_STATE)

    def out_map(cc):
        return block_diag(jnp.swapaxes(cc, 2, 3).reshape(n, 2, hg * S5_STATE, S5_GROUP), S5_STATE, S5_GROUP)

    bcast = lambda a: jnp.broadcast_to(a.reshape(n, 1, S5_LANES), (n, bsz, S5_LANES))
    r = jnp.arange(bsz * S5_PERM_T)
    perm = (r[None, :] == ((r % bsz) * S5_PERM_T + r // bsz)[:, None]).astype(BF16)
    stacked = [in_map(bbar_re), in_map(bbar_im), out_map(c_re), out_map(c_im), bcast(abar_re), bcast(abar_im),
               d_skip.reshape(n, 1, -1), w_glu.astype(BF16), b_glu.reshape(n, 1, -1)]
    return stacked, [perm[None], perm.T[None]]


def _odd_mix_kernel(ys_ref, h_ref, g_ref, w_ref, lng_ref, lnb_ref, ws_ref, bs_ref, wout_ref, fg_ref,
                    o_ref, y_ref, ug_ref, vn_ref, *, tt, final_norm):
    xn = _rms_rows(h_ref[...], g_ref[...]).astype(BF16)

    def proj(j):
        return jnp.dot(xn, w_ref[:, j * SG_WIDTH:(j + 1) * SG_WIDTH], preferred_element_type=F32)

    ug_ref[...] = (proj(0) * _silu(proj(2))).astype(BF16)
    vn_ref[...] = _layernorm_rows(proj(1), lng_ref[...], lnb_ref[...]).astype(BF16)

    y_ref[:, 0:S5_WIDTH] = ys_ref[...]
    for ci in range(tt // SG_CHUNK):
        rl = slice(ci * SG_CHUNK, (ci + 1) * SG_CHUNK)
        for hd in range(SG_HEADS):
            ll = slice(hd * SG_HD, (hd + 1) * SG_HD)
            sv = jnp.dot(ws_ref[hd], vn_ref[rl, ll], preferred_element_type=F32) + bs_ref[:, ll]
            y_ref[rl, S5_WIDTH + hd * SG_HD:S5_WIDTH + (hd + 1) * SG_HD] = (
                ug_ref[rl, ll].astype(F32) * sv).astype(BF16)
    out = h_ref[...] + jnp.dot(y_ref[...], wout_ref[...], preferred_element_type=F32)
    if final_norm:
        out = _rms_rows(out, fg_ref[...])
    o_ref[...] = out


def _odd_mix(ys, h, consts, layer, *, tt, final_norm):
    bsz, seq, _ = h.shape
    row = lambda c: pl.BlockSpec((None, tt, c), lambda b, i: (b, i, 0))
    return pl.pallas_call(
        functools.partial(_odd_mix_kernel, tt=tt, final_norm=final_norm),
        grid=(bsz, seq // tt),
        in_specs=[row(S5_WIDTH), row(D_MODEL)] + _layer_specs(consts, layer),
        out_specs=row(D_MODEL),
        out_shape=jax.ShapeDtypeStruct(h.shape, F32),
        scratch_shapes=[pltpu.VMEM((tt, S5_WIDTH + SG_WIDTH), BF16), pltpu.VMEM((tt, SG_WIDTH), BF16),
                        pltpu.VMEM((tt, SG_WIDTH), BF16)],
        compiler_params=pltpu.CompilerParams(dimension_semantics=("arbitrary", "arbitrary"),
                                             vmem_limit_bytes=VMEM_LIMIT),
        name="odd_mix",
    )(ys, h, *consts[0], *consts[1])


def _odd_layer(h, s5_consts, mix_consts, layer, *, bsz, seq, final_norm):
    ys = _s5(h, s5_consts, layer, bsz=bsz, seq=seq, tt=S5_TT)
    return _odd_mix(ys, h, mix_consts, layer, tt=ODD_TT, final_norm=final_norm)


def kernel(x, norm_g, final_g, e_w_in, e_w_a2, e_b_a, e_gla_g, e_conv_w, e_conv_b, e_cln_g, e_cln_b, e_w_out, o_w_in, o_lam_re, o_lam_im, o_log_dt, o_b_re, o_b_im, o_c_re, o_c_im, o_d, o_w_glu, o_b_glu, o_sg_ln_g, o_sg_ln_b, o_w_s, o_b_s, o_w_out):
    bsz, seq, d = x.shape
    assert d == D_MODEL and bsz == SUBLANE and seq % max(EVEN_TT * EVEN_SUB, ODD_TT) == 0
    depth = norm_g.shape[0]
    assert depth % 2 == 0
    n_odd = depth // 2

    even_consts = _even_params(norm_g[0::2], e_w_in, e_w_a2, e_b_a, e_gla_g, e_conv_w, e_conv_b, e_cln_g, e_cln_b,
                               e_w_out)
    s5_consts = _s5_params(o_lam_re, o_lam_im, o_log_dt, o_b_re, o_b_im, o_c_re, o_c_im, o_d, o_w_glu, o_b_glu, bsz)
    vec = lambda a: a.reshape(n_odd, 1, -1)
    o_g, o_lng, o_lnb = vec(norm_g[1::2]), vec(o_sg_ln_g), vec(o_sg_ln_b)
    o_w_s5 = o_w_in[:, :, :O_GU0].astype(BF16)
    o_w_sg = o_w_in[:, :, O_GU0:].astype(BF16)
    causal = jnp.tril(jnp.ones((SG_CHUNK, SG_CHUNK), dtype=bool))
    ws = jnp.where(causal, o_w_s, 0.0).astype(BF16)
    bs = jnp.repeat(jnp.swapaxes(o_b_s, 1, 2), SG_HD, axis=2)
    o_wout = o_w_out.astype(BF16)
    s5_consts = ([o_g, o_w_s5] + s5_consts[0], s5_consts[1])
    mix_consts = ([o_g, o_w_sg, o_lng, o_lnb, ws, bs, o_wout], [final_g.reshape(1, 1, -1)])

    h = x
    for layer in range(depth):
        i = layer // 2
        if layer % 2 == 0:
            h = _even_layer(h, even_consts, i, bsz=bsz, seq=seq)
        else:
            h = _odd_layer(h, s5_consts, mix_consts, i, bsz=bsz, seq=seq, final_norm=layer == depth - 1)
    return h
```

```python
import functools
import math

import jax
import jax.numpy as jnp
from jax import lax
from jax.experimental import pallas as pl
from jax.experimental.pallas import tpu as pltpu

F32 = jnp.float32
BF16 = jnp.bfloat16

EPS = 1e-6
D_MODEL = 1024

GLA_HEADS = 4
GLA_DK = D_MODEL // 2
GLA_DV = D_MODEL
GLA_HK = GLA_DK // GLA_HEADS
GLA_HV = GLA_DV // GLA_HEADS
GLA_RANK = 16
GLA_TAU = 16.0
GLA_CHUNK = 64

CONV_WIDTH = D_MODEL
CONV_K = 31
CONV_HALO = 32

S5_WIDTH = D_MODEL // 2
S5_GROUP = 16
S5_GROUPS = S5_WIDTH // S5_GROUP
S5_STATE = 64
S5_LANES = S5_GROUPS * S5_STATE
S5_HALF_IN = S5_WIDTH // 2
S5_HALF_ST = S5_LANES // 2

SG_WIDTH = D_MODEL
SG_HEADS = 8
SG_HD = SG_WIDTH // SG_HEADS
SG_CHUNK = 128

LANE = 128
SUBLANE = 8
A_LOW_PAD = LANE
VMEM_LIMIT = 56 * 1024 * 1024
S5_TT = 128
ODD_TT = 1024


def _silu(x):
    return x * jax.nn.sigmoid(x)


def _rms_rows(x, g):
    return x * lax.rsqrt(jnp.mean(x * x, axis=-1, keepdims=True) + EPS) * g


def _layernorm_rows(x, g, b):
    mu = jnp.mean(x, axis=-1, keepdims=True)
    xc = x - mu
    var = jnp.mean(xc * xc, axis=-1, keepdims=True)
    return xc * lax.rsqrt(var + EPS) * g + b


def _log_sigmoid(x):
    return jnp.minimum(x, 0.0) - jnp.log1p(jnp.exp(-jnp.abs(x)))


def _gelu_tanh(x):
    c = math.sqrt(2.0 / math.pi)
    return 0.5 * x * (1.0 + jnp.tanh(c * (x + 0.044715 * (x * x * x))))


def _layer_spec(a, i):
    nd = a.ndim - 1
    return pl.BlockSpec((None,) + a.shape[1:], lambda *_: (i,) + (0,) * nd, pipeline_mode=pl.Buffered(1))


def _layer_specs(consts, layer):
    stacked, shared = consts
    return [_layer_spec(a, layer) for a in stacked] + [_layer_spec(a, 0) for a in shared]


CONV_ROWS = 64
CONV_SEG = 32
EVEN_TT = CONV_SEG * SUBLANE
CONV_EXT_ROWS = (CONV_HALO + CONV_SEG) * SUBLANE
EVEN_SUB = 4
EVEN_GROUPS = EVEN_TT // CONV_ROWS
E_A0 = 2 * GLA_DK + 2 * GLA_DV
E_CV0 = E_A0 + GLA_RANK
E_CZ0 = E_CV0 + 2 * CONV_WIDTH
P_QK, P_V, P_Z, P_CZ = range(EVEN_GROUPS)


def _conv_row_maps():
    te, seg = jnp.divmod(jnp.arange(CONV_EXT_ROWS), SUBLANE)
    src = seg * CONV_SEG + te - CONV_HALO
    sel_cur = (src[:, None] == jnp.arange(EVEN_TT)[None, :]).astype(BF16)
    sel_prev = ((src + CONV_HALO)[:, None] == jnp.arange(CONV_HALO)[None, :]).astype(BF16)
    ts, sg = jnp.divmod(jnp.arange(EVEN_TT), SUBLANE)
    unsel = (jnp.arange(EVEN_TT)[:, None] == (sg * CONV_SEG + ts)[None, :]).astype(BF16)
    return sel_cur, sel_prev, unsel


def _even_kernel(h_ref, *refs, tt, nsub):
    n_in = 15
    o_ref, st_ref, tail_ref = refs[n_in], refs[n_in + 1], refs[n_in + 3]

    @pl.when(pl.program_id(1) == 0)
    def _():
        st_ref[...] = jnp.zeros_like(st_ref)
        tail_ref[...] = jnp.zeros_like(tail_ref)

    def tile(sub, carry):
        rows = pl.ds(pl.multiple_of(sub * tt, tt), tt)
        _even_tile(h_ref.at[rows], *refs[:n_in], o_ref.at[rows], *refs[n_in + 1:], tt=tt)
        return carry

    lax.fori_loop(0, nsub, tile, 0)


def _even_tile(h_ref, g_ref, wc_ref, wl_ref, wm_ref, wa2_ref, ba_ref, glag_ref, cw_ref, cb_ref, lng_ref,
               lnb_ref, wout_ref, selc_ref, selp_ref, unsel_ref,
               o_ref, st_ref, y_ref, tail_ref, ext_ref, conv_ref, xb_ref, xn_ref, p_ref, la_ref, *, tt):
    xn = _rms_rows(h_ref[...], g_ref[...]).astype(BF16)
    xn_ref[...] = xn

    c_val = jnp.dot(xn, wc_ref[:, 0:CONV_WIDTH], preferred_element_type=F32)
    c_gate = jnp.dot(xn, wc_ref[:, CONV_WIDTH:2 * CONV_WIDTH], preferred_element_type=F32)
    u = (c_val * jax.nn.sigmoid(c_gate)).astype(BF16)
    ext_ref[...] = (jnp.dot(selc_ref[...], u, preferred_element_type=F32)
                    + jnp.dot(selp_ref[...], tail_ref[...], preferred_element_type=F32))
    tail_ref[...] = u[tt - CONV_HALO:tt, :]

    a_low = jnp.dot(xn, wl_ref[...], preferred_element_type=F32).astype(BF16)
    logit = jnp.dot(a_low, wa2_ref[...], preferred_element_type=F32) + ba_ref[...]
    la_ref[...] = _log_sigmoid(logit) * (1.0 / GLA_TAU)

    off0 = CONV_HALO - (CONV_K - 1)
    rsub = CONV_ROWS // SUBLANE

    def conv_block(rb, carry):
        p_ref[rb] = jnp.dot(xn_ref[...], wm_ref[rb], preferred_element_type=F32).astype(BF16)
        r0 = pl.multiple_of(rb * CONV_ROWS, CONV_ROWS)
        for lb in range(CONV_WIDTH // LANE):
            ll = slice(lb * LANE, (lb + 1) * LANE)
            acc = [jnp.zeros((SUBLANE, LANE), F32) for _ in range(rsub)]
            for e in range(CONV_K + rsub - 1):
                x = ext_ref[pl.ds(r0 + (off0 + e) * SUBLANE, SUBLANE), ll]
                for j in range(rsub):
                    kk = e - j
                    if 0 <= kk < CONV_K:
                        acc[j] = acc[j] + x * cw_ref[kk * SUBLANE:(kk + 1) * SUBLANE, ll]
            for j in range(rsub):
                conv_ref[pl.ds(r0 + j * SUBLANE, SUBLANE), ll] = acc[j]
        x = conv_ref[pl.ds(r0, CONV_ROWS), :] + cb_ref[...]
        xb_ref[pl.ds(r0, CONV_ROWS), :] = _silu(_layernorm_rows(x, lng_ref[...], lnb_ref[...])).astype(BF16)
        return carry

    lax.fori_loop(0, EVEN_GROUPS, conv_block, 0)
    y_conv = jnp.dot(unsel_ref[...], xb_ref[...], preferred_element_type=F32)
    y_ref[:, GLA_DV:GLA_DV + CONV_WIDTH] = (y_conv * _silu(p_ref[P_CZ].astype(F32))).astype(BF16)

    c = GLA_CHUNK
    nc = tt // c
    rows = lax.broadcasted_iota(jnp.int32, (tt, tt), 0)
    cols = lax.broadcasted_iota(jnp.int32, (tt, tt), 1)
    causal = (rows >= cols) & ((rows // c) == (cols // c))
    tri = causal.astype(BF16)
    scale = GLA_HK ** -0.5
    nt_dims = (((1,), (1,)), ((), ()))
    tn_dims = (((0,), (0,)), ((), ()))

    g = la_ref[...]
    g_hi = g.astype(BF16)
    r1 = g - g_hi.astype(F32)
    g_mid = r1.astype(BF16)
    g_lo = (r1 - g_mid.astype(F32)).astype(BF16)
    b = (jnp.dot(tri, g_hi, preferred_element_type=F32) + jnp.dot(tri, g_mid, preferred_element_type=F32)
         + jnp.dot(tri, g_lo, preferred_element_type=F32))
    b_last = [b[ci * c + c - 1:ci * c + c, :] for ci in range(nc)]
    la_ref[...] = b
    decay = [jnp.exp(bl) for bl in b_last]

    for hd in range(GLA_HEADS):
        kl = slice(hd * GLA_HK, (hd + 1) * GLA_HK)
        vl = slice(hd * GLA_HV, (hd + 1) * GLA_HV)
        bh = la_ref[:, kl]
        bh_last = jnp.concatenate([jnp.broadcast_to(bl[:, kl], (c, GLA_HK)) for bl in b_last], axis=0)
        qh = p_ref[P_QK, :, kl].astype(F32)
        kh = p_ref[P_QK, :, GLA_DK + hd * GLA_HK:GLA_DK + (hd + 1) * GLA_HK].astype(F32)
        qf = (qh * jnp.exp(bh) * scale).astype(BF16)
        k_intra = (kh * jnp.exp(-bh)).astype(BF16)
        k_state = (kh * jnp.exp(bh_last - bh)).astype(BF16)
        v = p_ref[P_V, :, vl]
        att = lax.dot_general(qf, k_intra, nt_dims, preferred_element_type=F32)
        att = jnp.where(causal, att, 0.0).astype(BF16)
        o_intra = jnp.dot(att, v, preferred_element_type=F32)
        s_t = st_ref[hd]
        o_inter = []
        for ci in range(nc):
            rl = slice(ci * c, (ci + 1) * c)
            o_inter.append(lax.dot_general(qf[rl, :], s_t.astype(BF16), nt_dims, preferred_element_type=F32))
            kv_t = lax.dot_general(v[rl, :], k_state[rl, :], tn_dims, preferred_element_type=F32)
            s_t = s_t * decay[ci][:, kl] + kv_t
        st_ref[hd] = s_t
        o = o_intra + jnp.concatenate(o_inter, axis=0)
        o = o * lax.rsqrt(jnp.mean(o * o, axis=-1, keepdims=True) + EPS) * glag_ref[:, vl]
        y_ref[:, vl] = (o * _silu(p_ref[P_Z, :, vl].astype(F32))).astype(BF16)

    o_ref[...] = h_ref[...] + jnp.dot(y_ref[...], wout_ref[...], preferred_element_type=F32)


def _even_layer(h, consts, layer, *, bsz, seq):
    tt = EVEN_TT
    row = pl.BlockSpec((None, tt * EVEN_SUB, D_MODEL), lambda b, i: (b, i, 0))
    return pl.pallas_call(
        functools.partial(_even_kernel, tt=tt, nsub=EVEN_SUB),
        grid=(bsz, seq // (tt * EVEN_SUB)),
        in_specs=[row] + _layer_specs(consts, layer),
        out_specs=row,
        out_shape=jax.ShapeDtypeStruct(h.shape, F32),
        scratch_shapes=[pltpu.VMEM((GLA_HEADS, GLA_HV, GLA_HK), F32),
                        pltpu.VMEM((tt, GLA_DV + CONV_WIDTH), BF16),
                        pltpu.VMEM((CONV_HALO, CONV_WIDTH), BF16),
                        pltpu.VMEM((CONV_EXT_ROWS, CONV_WIDTH), F32),
                        pltpu.VMEM((tt, CONV_WIDTH), F32),
                        pltpu.VMEM((tt, CONV_WIDTH), BF16),
                        pltpu.VMEM((tt, D_MODEL), BF16),
                        pltpu.VMEM((EVEN_GROUPS, tt, D_MODEL), BF16),
                        pltpu.VMEM((tt, GLA_DK), F32)],
        compiler_params=pltpu.CompilerParams(dimension_semantics=("arbitrary", "arbitrary"),
                                             vmem_limit_bytes=VMEM_LIMIT),
        name="even_layer",
    )(h, *consts[0], *consts[1])


def _even_params(norm_g, w_in, w_a2, b_a, gla_g, conv_w, conv_b, cln_g, cln_b, w_out):
    n = w_in.shape[0]
    assert EVEN_GROUPS == 4
    wm = jnp.stack([w_in[:, :, 0:2 * GLA_DK], w_in[:, :, 2 * GLA_DK:2 * GLA_DK + GLA_DV],
                    w_in[:, :, 2 * GLA_DK + GLA_DV:E_A0], w_in[:, :, E_CZ0:]], axis=1).astype(BF16)
    wl = jnp.pad(w_in[:, :, E_A0:E_CV0], ((0, 0), (0, 0), (0, A_LOW_PAD - GLA_RANK))).astype(BF16)
    wc = w_in[:, :, E_CV0:E_CZ0].astype(BF16)
    wa2 = jnp.pad(w_a2, ((0, 0), (0, A_LOW_PAD - GLA_RANK), (0, 0))).astype(BF16)
    cw = jnp.repeat(conv_w, SUBLANE, axis=1)
    wout = w_out.astype(BF16)
    maps = _conv_row_maps()
    vec = lambda a: a.reshape(n, 1, -1)
    stacked = [vec(norm_g), wc, wl, wm, wa2, vec(b_a), vec(gla_g), cw, vec(conv_b), vec(cln_g), vec(cln_b), wout]
    return stacked, [m[None] for m in maps]


O_GU0 = 2 * S5_WIDTH
S5_SCAN_LANES = 512
S5_PERM_T = 32


def _s5_kernel(h_ref, g_ref, w_ref, bre_ref, bim_ref, cre_ref, cim_ref, are_ref, aim_ref, d_ref, wglu_ref,
               bglu_ref, perm_ref, permt_ref, o_ref, u_ref, sz_ref, xre_ref, xim_ref, sre_ref, sim_ref, *, bsz, tt):
    @pl.when(pl.program_id(0) == 0)
    def _():
        sre_ref[...] = jnp.zeros_like(sre_ref)
        sim_ref[...] = jnp.zeros_like(sim_ref)

    xn = _rms_rows(h_ref[...].reshape(bsz * tt, D_MODEL), g_ref[...]).astype(BF16)
    u_ref[...] = jnp.dot(xn, w_ref[:, 0:S5_WIDTH], preferred_element_type=F32).astype(BF16).reshape(
        bsz, tt, S5_WIDTH)
    sz_ref[...] = _silu(jnp.dot(xn, w_ref[:, S5_WIDTH:2 * S5_WIDTH], preferred_element_type=F32)).astype(
        BF16).reshape(bsz, tt, S5_WIDTH)

    pt = S5_PERM_T
    prow = bsz * pt
    u = jnp.concatenate(
        [jnp.dot(perm_ref[...], u_ref[:, s * pt:(s + 1) * pt, :].reshape(prow, S5_WIDTH),
                 preferred_element_type=F32).astype(BF16) for s in range(tt // pt)], axis=0)
    for hf in range(2):
        ul = u[:, hf * S5_HALF_IN:(hf + 1) * S5_HALF_IN]
        sl = slice(hf * S5_HALF_ST, (hf + 1) * S5_HALF_ST)
        xre_ref[:, sl] = jnp.dot(ul, bre_ref[hf], preferred_element_type=F32)
        xim_ref[:, sl] = jnp.dot(ul, bim_ref[hf], preferred_element_type=F32)

    for cb in range(S5_LANES // S5_SCAN_LANES):
        ll = slice(cb * S5_SCAN_LANES, (cb + 1) * S5_SCAN_LANES)
        a_re = are_ref[:, ll]
        a_im = aim_ref[:, ll]
        x_re = sre_ref[:, ll]
        x_im = sim_ref[:, ll]
        for t in range(tt):
            idx = slice(t * bsz, (t + 1) * bsz)
            x_re, x_im = (a_re * x_re - a_im * x_im + xre_ref[idx, ll],
                          a_re * x_im + a_im * x_re + xim_ref[idx, ll])
            xre_ref[idx, ll] = x_re
            xim_ref[idx, ll] = x_im
        sre_ref[:, ll] = x_re
        sim_ref[:, ll] = x_im

    ys = []
    for hf in range(2):
        sl = slice(hf * S5_HALF_ST, (hf + 1) * S5_HALF_ST)
        y = jnp.dot(xre_ref[:, sl].astype(BF16), cre_ref[hf], preferred_element_type=F32)
        y = y - jnp.dot(xim_ref[:, sl].astype(BF16), cim_ref[hf], preferred_element_type=F32)
        ys.append(y)
    y = jnp.concatenate(ys, axis=-1) + d_ref[...] * u.astype(F32)
    y = _gelu_tanh(y)
    y = y * jax.nn.sigmoid(jnp.dot(y.astype(BF16), wglu_ref[...], preferred_element_type=F32) + bglu_ref[...])
    y = y.astype(BF16)
    for s in range(tt // pt):
        y_bt = jnp.dot(permt_ref[...], y[s * prow:(s + 1) * prow, :], preferred_element_type=F32)
        gate = sz_ref[:, s * pt:(s + 1) * pt, :].astype(F32)
        o_ref[:, s * pt:(s + 1) * pt, :] = (y_bt.reshape(bsz, pt, S5_WIDTH) * gate).astype(BF16)


def _s5(h3, consts, layer, *, bsz, seq, tt):
    blk = lambda c: pl.BlockSpec((bsz, tt, c), lambda i: (0, i, 0))
    return pl.pallas_call(
        functools.partial(_s5_kernel, bsz=bsz, tt=tt),
        grid=(seq // tt,),
        in_specs=[blk(D_MODEL)] + _layer_specs(consts, layer),
        out_specs=blk(S5_WIDTH),
        out_shape=jax.ShapeDtypeStruct((bsz, seq, S5_WIDTH), BF16),
        scratch_shapes=[pltpu.VMEM((bsz, tt, S5_WIDTH), BF16), pltpu.VMEM((bsz, tt, S5_WIDTH), BF16),
                        pltpu.VMEM((bsz * tt, S5_LANES), F32), pltpu.VMEM((bsz * tt, S5_LANES), F32),
                        pltpu.VMEM((bsz, S5_LANES), F32), pltpu.VMEM((bsz, S5_LANES), F32)],
        compiler_params=pltpu.CompilerParams(dimension_semantics=("arbitrary",),
                                             vmem_limit_bytes=VMEM_LIMIT),
        name="s5",
    )(h3, *consts[0], *consts[1])


def _s5_params(lam_re, lam_im, log_dt, b_re, b_im, c_re, c_im, d_skip, w_glu, b_glu, bsz):
    n = lam_re.shape[0]
    dt = jnp.exp(log_dt.astype(F32))[..., None]
    mag = jnp.exp(lam_re * dt)
    abar_re = mag * jnp.cos(lam_im * dt)
    abar_im = mag * jnp.sin(lam_im * dt)
    den = lam_re * lam_re + lam_im * lam_im
    nr, ni = abar_re - 1.0, abar_im
    coef_re = (nr * lam_re + ni * lam_im) / den
    coef_im = (ni * lam_re - nr * lam_im) / den
    bbar_re = coef_re[..., None] * b_re - coef_im[..., None] * b_im
    bbar_im = coef_re[..., None] * b_im + coef_im[..., None] * b_re
    hg = S5_GROUPS // 2

    def block_diag(rows, row_group, lane_group):
        tiled = jnp.tile(rows, (1, 1, 1, hg))
        keep = ((jnp.arange(hg * row_group) // row_group)[:, None]
                == (jnp.arange(hg * lane_group) // lane_group)[None, :])
        return jnp.where(keep, tiled, 0.0).astype(BF16)

    def in_map(bb):
        return block_diag(jnp.swapaxes(bb, 2, 3).reshape(n, 2, hg * S5_GROUP, S5_STATE), S5_GROUP, S5_STATE)

    def out_map(cc):
        return block_diag(jnp.swapaxes(cc, 2, 3).reshape(n, 2, hg * S5_STATE, S5_GROUP), S5_STATE, S5_GROUP)

    bcast = lambda a: jnp.broadcast_to(a.reshape(n, 1, S5_LANES), (n, bsz, S5_LANES))
    r = jnp.arange(bsz * S5_PERM_T)
    perm = (r[None, :] == ((r % bsz) * S5_PERM_T + r // bsz)[:, None]).astype(BF16)
    stacked = [in_map(bbar_re), in_map(bbar_im), out_map(c_re), out_map(c_im), bcast(abar_re), bcast(abar_im),
               d_skip.reshape(n, 1, -1), w_glu.astype(BF16), b_glu.reshape(n, 1, -1)]
    return stacked, [perm[None], perm.T[None]]


def _odd_mix_kernel(ys_ref, h_ref, g_ref, w_ref, lng_ref, lnb_ref, ws_ref, bs_ref, wout_ref, fg_ref,
                    o_ref, y_ref, ug_ref, vn_ref, *, tt, final_norm):
    xn = _rms_rows(h_ref[...], g_ref[...]).astype(BF16)

    def proj(j):
        return jnp.dot(xn, w_ref[:, j * SG_WIDTH:(j + 1) * SG_WIDTH], preferred_element_type=F32)

    ug_ref[...] = (proj(0) * _silu(proj(2))).astype(BF16)
    vn_ref[...] = _layernorm_rows(proj(1), lng_ref[...], lnb_ref[...]).astype(BF16)

    y_ref[:, 0:S5_WIDTH] = ys_ref[...]
    for ci in range(tt // SG_CHUNK):
        rl = slice(ci * SG_CHUNK, (ci + 1) * SG_CHUNK)
        for hd in range(SG_HEADS):
            ll = slice(hd * SG_HD, (hd + 1) * SG_HD)
            sv = jnp.dot(ws_ref[hd], vn_ref[rl, ll], preferred_element_type=F32) + bs_ref[:, ll]
            y_ref[rl, S5_WIDTH + hd * SG_HD:S5_WIDTH + (hd + 1) * SG_HD] = (
                ug_ref[rl, ll].astype(F32) * sv).astype(BF16)
    out = h_ref[...] + jnp.dot(y_ref[...], wout_ref[...], preferred_element_type=F32)
    if final_norm:
        out = _rms_rows(out, fg_ref[...])
    o_ref[...] = out


def _odd_mix(ys, h, consts, layer, *, tt, final_norm):
    bsz, seq, _ = h.shape
    row = lambda c: pl.BlockSpec((None, tt, c), lambda b, i: (b, i, 0))
    return pl.pallas_call(
        functools.partial(_odd_mix_kernel, tt=tt, final_norm=final_norm),
        grid=(bsz, seq // tt),
        in_specs=[row(S5_WIDTH), row(D_MODEL)] + _layer_specs(consts, layer),
        out_specs=row(D_MODEL),
        out_shape=jax.ShapeDtypeStruct(h.shape, F32),
        scratch_shapes=[pltpu.VMEM((tt, S5_WIDTH + SG_WIDTH), BF16), pltpu.VMEM((tt, SG_WIDTH), BF16),
                        pltpu.VMEM((tt, SG_WIDTH), BF16)],
        compiler_params=pltpu.CompilerParams(dimension_semantics=("arbitrary", "arbitrary"),
                                             vmem_limit_bytes=VMEM_LIMIT),
        name="odd_mix",
    )(ys, h, *consts[0], *consts[1])


def _odd_layer(h, s5_consts, mix_consts, layer, *, bsz, seq, final_norm):
    ys = _s5(h, s5_consts, layer, bsz=bsz, seq=seq, tt=S5_TT)
    return _odd_mix(ys, h, mix_consts, layer, tt=ODD_TT, final_norm=final_norm)


def kernel(x, norm_g, final_g, e_w_in, e_w_a2, e_b_a, e_gla_g, e_conv_w, e_conv_b, e_cln_g, e_cln_b, e_w_out, o_w_in, o_lam_re, o_lam_im, o_log_dt, o_b_re, o_b_im, o_c_re, o_c_im, o_d, o_w_glu, o_b_glu, o_sg_ln_g, o_sg_ln_b, o_w_s, o_b_s, o_w_out):
    bsz, seq, d = x.shape
    assert d == D_MODEL and bsz == SUBLANE and seq % max(EVEN_TT * EVEN_SUB, ODD_TT) == 0
    depth = norm_g.shape[0]
    assert depth % 2 == 0
    n_odd = depth // 2

    even_consts = _even_params(norm_g[0::2], e_w_in, e_w_a2, e_b_a, e_gla_g, e_conv_w, e_conv_b, e_cln_g, e_cln_b,
                               e_w_out)
    s5_consts = _s5_params(o_lam_re, o_lam_im, o_log_dt, o_b_re, o_b_im, o_c_re, o_c_im, o_d, o_w_glu, o_b_glu, bsz)
    vec = lambda a: a.reshape(n_odd, 1, -1)
    o_g, o_lng, o_lnb = vec(norm_g[1::2]), vec(o_sg_ln_g), vec(o_sg_ln_b)
    o_w_s5 = o_w_in[:, :, :O_GU0].astype(BF16)
    o_w_sg = o_w_in[:, :, O_GU0:].astype(BF16)
    causal = jnp.tril(jnp.ones((SG_CHUNK, SG_CHUNK), dtype=bool))
    ws = jnp.where(causal, o_w_s, 0.0).astype(BF16)
    bs = jnp.repeat(jnp.swapaxes(o_b_s, 1, 2), SG_HD, axis=2)
    o_wout = o_w_out.astype(BF16)
    s5_consts = ([o_g, o_w_s5] + s5_consts[0], s5_consts[1])
    mix_consts = ([o_g, o_w_sg, o_lng, o_lnb, ws, bs, o_wout], [final_g.reshape(1, 1, -1)])

    h = x
    for layer in range(depth):
        i = layer // 2
        if layer % 2 == 0:
            h = _even_layer(h, even_consts, i, bsz=bsz, seq=seq)
        else:
            h = _odd_layer(h, s5_consts, mix_consts, i, bsz=bsz, seq=seq, final_norm=layer == depth - 1)
    return h
```
